```python
import jax, jax.numpy as jnp
from jax import lax
import numpy as np

D_MODEL = 1024
BATCH = 8
SEQ = 4096
DEPTH = 1

D_RNN = 512
RNN_BLOCKS = 8
RNN_BLOCK_DIM = D_RNN // RNN_BLOCKS
CONV_WIDTH = 4
LRU_C = 8.0
N_HEADS = 8
N_KV = 2
HPG = N_HEADS // N_KV
HEAD_DIM = 64
D_ATTN = N_HEADS * HEAD_DIM
D_MIX = D_RNN + D_ATTN
KV_W = N_KV * HEAD_DIM
CMP_LEN = 32
CMP_STRIDE = 16
CMP_HIDDEN = 256
SLC_BLK = 64
SLC_TOPN = 16
WIN = 512
WIN_Q_BLK = 128
SLC_Q_BLK = 64
FORCE_SCORE = 1.0e4
ROPE_THETA = 500000.0
ROPE_DIM = HEAD_DIM // 4
D_FF = -(-8 * D_MODEL // (3 * 256)) * 256
EPS = 1e-6
IN_SIZES = (D_RNN, D_RNN, D_ATTN, KV_W, KV_W, KV_W, KV_W, KV_W, KV_W, 3 * N_HEADS)
D_IN = sum(IN_SIZES)
SPLIT_POINTS = tuple(int(v) for v in np.cumsum(IN_SIZES)[:-1])

kernel_name = "hymba_rglru_nsa_hybrid"


def rms_norm(x, w):
    xf = x.astype(jnp.float32)
    y = xf * lax.rsqrt(jnp.mean(xf * xf, axis=-1, keepdims=True) + EPS)
    return (y * w.astype(jnp.float32)).astype(x.dtype)


def partial_rope(x, pos):
    half = ROPE_DIM // 2
    inv_freq = jnp.power(ROPE_THETA, -jnp.arange(half, dtype=jnp.float32) * 2.0 / ROPE_DIM)
    ang = pos.astype(jnp.float32)[..., None] * inv_freq
    cos = jnp.cos(ang)[:, :, None, :]
    sin = jnp.sin(ang)[:, :, None, :]
    xf = x.astype(jnp.float32)
    x1 = xf[..., :half]
    x2 = xf[..., half:ROPE_DIM]
    out = jnp.concatenate([x1 * cos - x2 * sin, x2 * cos + x1 * sin, xf[..., ROPE_DIM:]], axis=-1)
    return out.astype(x.dtype)


def masked_softmax(s, mask):
    s = jnp.where(mask, s.astype(jnp.float32), -jnp.inf)
    m = jnp.max(s, axis=-1, keepdims=True)
    m = jnp.where(jnp.isfinite(m), m, 0.0)
    e = jnp.where(mask, jnp.exp(s - m), 0.0)
    return e / jnp.maximum(jnp.sum(e, axis=-1, keepdims=True), 1e-30)


def block_diag_linear(x, w, b):
    B, T, _ = x.shape
    xb = x.reshape(B, T, RNN_BLOCKS, RNN_BLOCK_DIM)
    return jnp.einsum('btnd,nde->btne', xb, w).reshape(B, T, D_RNN) + b


def lru_combine(c1, c2):
    a1, b1 = c1
    a2, b2 = c2
    return a1 * a2, a2 * b1 + b2


def rglru_mixer(xr, gr, conv_w, conv_b, gate_a_w, gate_a_b, gate_x_w, gate_x_b, lru_lambda):
    T = xr.shape[1]
    xp = jnp.pad(xr, ((0, 0), (CONV_WIDTH - 1, 0), (0, 0)))
    xc = conv_b
    for k in range(CONV_WIDTH):
        xc = xc + conv_w[k] * xp[:, k:k + T]
    r = jax.nn.sigmoid(block_diag_linear(xc, gate_a_w, gate_a_b)).astype(jnp.float32)
    i = jax.nn.sigmoid(block_diag_linear(xc, gate_x_w, gate_x_b))
    log_a = -LRU_C * r * jax.nn.softplus(-lru_lambda.astype(jnp.float32))
    a = jnp.exp(log_a)
    b = jnp.sqrt(-jnp.expm1(2.0 * log_a)) * (i * xc).astype(jnp.float32)
    _, h = lax.associative_scan(lru_combine, (a, b), axis=1)
    return jax.nn.gelu(gr) * h.astype(xr.dtype)


def nsa_mixer(q, kc_tok, vc_tok, ks_tok, vs_tok, kw_tok, vw_tok, gate_logits, positions,
              q_norm_w, k_norm_w, cmp_pos, cmp_k_w1, cmp_k_w2, cmp_v_w1, cmp_v_w2):
    B, T, _ = q.shape
    scale = HEAD_DIM ** -0.5
    t_idx = jnp.arange(T)

    def kv(t):
        return t.reshape(B, T, N_KV, HEAD_DIM)

    q = partial_rope(rms_norm(q.reshape(B, T, N_HEADS, HEAD_DIM), q_norm_w), positions)
    qg = q.reshape(B, T, N_KV, HPG, HEAD_DIM)

    n_cmp = (T - CMP_LEN) // CMP_STRIDE + 1
    cmp_start = jnp.arange(n_cmp) * CMP_STRIDE
    tok_idx = cmp_start[:, None] + jnp.arange(CMP_LEN)[None, :]
    cmp_end = cmp_start + CMP_LEN - 1

    def compress(t, w1, w2):
        blk = t[:, tok_idx] + cmp_pos[None, None, :, None, :]
        blk = blk.transpose(0, 1, 3, 2, 4).reshape(B, n_cmp, N_KV, CMP_LEN * HEAD_DIM)
        return jax.nn.gelu(blk @ w1) @ w2

    kc = partial_rope(rms_norm(compress(kv(kc_tok), cmp_k_w1, cmp_k_w2), k_norm_w), positions[:, cmp_end])
    vc = compress(kv(vc_tok), cmp_v_w1, cmp_v_w2)
    s_c = jnp.einsum('btghd,bcgd->bghtc', qg, kc).astype(jnp.float32) * scale
    p_c = masked_softmax(s_c, cmp_end[None, :] <= t_idx[:, None])
    o_cmp = jnp.einsum('bghtc,bcgd->btghd', p_c.astype(vc.dtype), vc)

    nb = T // SLC_BLK
    blk_start = jnp.arange(nb) * SLC_BLK
    lo = jnp.maximum(cmp_start[:, None], blk_start[None, :])
    hi = jnp.minimum(cmp_start[:, None] + CMP_LEN, blk_start[None, :] + SLC_BLK)
    cover = jnp.clip(hi - lo, 0).astype(jnp.float32) / CMP_LEN
    imp = jnp.einsum('bghtc,cj->bgtj', p_c, cover)
    cur = t_idx // SLC_BLK
    jb = jnp.arange(nb)
    valid_blk = blk_start[None, :] <= t_idx[:, None]
    forced = (jb[None, :] == 0) | (jb[None, :] == cur[:, None]) | (jb[None, :] == cur[:, None] - 1)
    score = jnp.where(forced, FORCE_SCORE, jnp.where(valid_blk, imp, -jnp.inf))
    n_top = min(SLC_TOPN, nb)
    _, sel_idx = lax.top_k(score, n_top)

    ks = partial_rope(rms_norm(kv(ks_tok), k_norm_w), positions)
    kb = ks.transpose(0, 2, 1, 3).reshape(B, N_KV, nb, SLC_BLK, HEAD_DIM)
    vb = kv(vs_tok).transpose(0, 2, 1, 3).reshape(B, N_KV, nb, SLC_BLK, HEAD_DIM)
    nqs = T // SLC_Q_BLK
    q_s = qg.reshape(B, nqs, SLC_Q_BLK, N_KV, HPG, HEAD_DIM).transpose(1, 0, 3, 4, 2, 5)
    i_s = sel_idx.reshape(B, N_KV, nqs, SLC_Q_BLK, n_top).transpose(2, 0, 1, 3, 4)
    t_s = t_idx.reshape(nqs, SLC_Q_BLK)
    b_ix = jnp.arange(B)[:, None, None, None]
    g_ix = jnp.arange(N_KV)[None, :, None, None]

    def sel_step(args):
        qb, ib, tb = args
        kg = kb[b_ix, g_ix, ib]
        vg = vb[b_ix, g_ix, ib]
        s = jnp.einsum('bghqd,bgqnkd->bghqnk', qb, kg).astype(jnp.float32) * scale
        key_pos = ib[..., None] * SLC_BLK + jnp.arange(SLC_BLK)
        mask = (key_pos <= tb[None, None, :, None, None])[:, :, None]
        p = masked_softmax(s.reshape(s.shape[:4] + (-1,)), mask.reshape(mask.shape[:4] + (-1,)))
        p = p.reshape(s.shape)
        return jnp.einsum('bghqnk,bgqnkd->bghqd', p.astype(vg.dtype), vg)

    o_s = lax.map(sel_step, (q_s, i_s, t_s))
    o_slc = o_s.transpose(1, 0, 4, 2, 3, 5).reshape(B, T, N_KV, HPG, HEAD_DIM)

    kw = partial_rope(rms_norm(kv(kw_tok), k_norm_w), positions)
    kwp = jnp.pad(kw.transpose(0, 2, 1, 3), ((0, 0), (0, 0), (WIN, 0), (0, 0)))
    vwp = jnp.pad(kv(vw_tok).transpose(0, 2, 1, 3), ((0, 0), (0, 0), (WIN, 0), (0, 0)))
    nqw = T // WIN_Q_BLK
    q_w = qg.reshape(B, nqw, WIN_Q_BLK, N_KV, HPG, HEAD_DIM).transpose(1, 0, 3, 4, 2, 5)
    starts = jnp.arange(nqw) * WIN_Q_BLK
    span = WIN + WIN_Q_BLK

    def win_step(args):
        qb, start = args
        kblk = lax.dynamic_slice_in_dim(kwp, start, span, axis=2)
        vblk = lax.dynamic_slice_in_dim(vwp, start, span, axis=2)
        s = jnp.einsum('bghqd,bgkd->bghqk', qb, kblk).astype(jnp.float32) * scale
        tq = start + jnp.arange(WIN_Q_BLK)
        sk = start - WIN + jnp.arange(span)
        mask = (sk[None, :] <= tq[:, None]) & (tq[:, None] - sk[None, :] < WIN) & (sk[None, :] >= 0)
        p = masked_softmax(s, mask)
        return jnp.einsum('bghqk,bgkd->bghqd', p.astype(vblk.dtype), vblk)

    o_w = lax.map(win_step, (q_w, starts))
    o_win = o_w.transpose(1, 0, 4, 2, 3, 5).reshape(B, T, N_KV, HPG, HEAD_DIM)

    g = jax.nn.sigmoid(gate_logits.astype(jnp.float32)).reshape(B, T, N_KV, HPG, 3).astype(q.dtype)
    o = g[..., 0:1] * o_cmp + g[..., 1:2] * o_slc + g[..., 2:3] * o_win
    return o.reshape(B, T, D_ATTN)


def setup_inputs(seed: int = 0) -> dict:
    key = jax.random.key(seed)
    ks = jax.random.split(key, 26)
    f32 = jnp.float32
    L = DEPTH

    def nrm(k, shape, scale):
        return jax.random.normal(k, shape, f32) * scale

    def gain(k, shape):
        return 1.0 + 0.01 * jax.random.normal(k, shape, f32)

    offset = jax.random.randint(ks[1], (BATCH, 1), 0, 1024, jnp.int32)
    positions = jnp.arange(SEQ, dtype=jnp.int32)[None, :] + offset
    u = jax.random.uniform(ks[10], (L, D_RNN), f32, 0.9, 0.999)
    s = u ** (1.0 / LRU_C)
    lru_lambda = jnp.log(s) - jnp.log1p(-s)
    return {
        "x": nrm(ks[0], (BATCH, SEQ, D_MODEL), 1.0),
        "positions": positions,
        "attn_norm_w": gain(ks[2], (L, D_MODEL)),
        "w_in": nrm(ks[3], (L, D_MODEL, D_IN), D_MODEL ** -0.5),
        "conv_w": nrm(ks[4], (L, CONV_WIDTH, D_RNN), CONV_WIDTH ** -0.5),
        "conv_b": nrm(ks[5], (L, D_RNN), 0.01),
        "gate_a_w": nrm(ks[6], (L, RNN_BLOCKS, RNN_BLOCK_DIM, RNN_BLOCK_DIM), RNN_BLOCK_DIM ** -0.5),
        "gate_a_b": nrm(ks[7], (L, D_RNN), 0.1),
        "gate_x_w": nrm(ks[8], (L, RNN_BLOCKS, RNN_BLOCK_DIM, RNN_BLOCK_DIM), RNN_BLOCK_DIM ** -0.5),
        "gate_x_b": nrm(ks[9], (L, D_RNN), 0.1),
        "lru_lambda": lru_lambda,
        "q_norm_w": gain(ks[11], (L, HEAD_DIM)),
        "k_norm_w": gain(ks[12], (L, HEAD_DIM)),
        "cmp_pos": nrm(ks[13], (L, CMP_LEN, HEAD_DIM), 0.1),
        "cmp_k_w1": nrm(ks[14], (L, CMP_LEN * HEAD_DIM, CMP_HIDDEN), (CMP_LEN * HEAD_DIM) ** -0.5),
        "cmp_k_w2": nrm(ks[15], (L, CMP_HIDDEN, HEAD_DIM), CMP_HIDDEN ** -0.5),
        "cmp_v_w1": nrm(ks[16], (L, CMP_LEN * HEAD_DIM, CMP_HIDDEN), (CMP_LEN * HEAD_DIM) ** -0.5),
        "cmp_v_w2": nrm(ks[17], (L, CMP_HIDDEN, HEAD_DIM), CMP_HIDDEN ** -0.5),
        "rnn_out_norm_w": gain(ks[18], (L, D_RNN)),
        "attn_out_norm_w": gain(ks[19], (L, D_ATTN)),
        "w_out": nrm(ks[20], (L, D_MIX, D_MODEL), D_MIX ** -0.5),
        "ffn_norm_w": gain(ks[21], (L, D_MODEL)),
        "w_gate": nrm(ks[22], (L, D_MODEL, D_FF), D_MODEL ** -0.5),
        "w_up": nrm(ks[23], (L, D_MODEL, D_FF), D_MODEL ** -0.5),
        "w_down": nrm(ks[24], (L, D_FF, D_MODEL), D_FF ** -0.5),
    }


def reference(x, positions, attn_norm_w, w_in, conv_w, conv_b, gate_a_w, gate_a_b, gate_x_w, gate_x_b,
              lru_lambda, q_norm_w, k_norm_w, cmp_pos, cmp_k_w1, cmp_k_w2, cmp_v_w1, cmp_v_w2,
              rnn_out_norm_w, attn_out_norm_w, w_out, ffn_norm_w, w_gate, w_up, w_down):
    for l in range(DEPTH):
        h = rms_norm(x, attn_norm_w[l])
        proj = h @ w_in[l]
        xr, gr, q, kc, vc, ks, vs, kw, vw, gl = jnp.split(proj, SPLIT_POINTS, axis=-1)
        y_rnn = rglru_mixer(xr, gr, conv_w[l], conv_b[l], gate_a_w[l], gate_a_b[l],
                            gate_x_w[l], gate_x_b[l], lru_lambda[l])
        y_att = nsa_mixer(q, kc, vc, ks, vs, kw, vw, gl, positions, q_norm_w[l], k_norm_w[l],
                          cmp_pos[l], cmp_k_w1[l], cmp_k_w2[l], cmp_v_w1[l], cmp_v_w2[l])
        y = jnp.concatenate([rms_norm(y_rnn, rnn_out_norm_w[l]), rms_norm(y_att, attn_out_norm_w[l])], axis=-1)
        x = x + y @ w_out[l]
        h = rms_norm(x, ffn_norm_w[l])
        x = x + (jax.nn.silu(h @ w_gate[l]) * (h @ w_up[l])) @ w_down[l]
    return x
```

```python
import functools

import numpy as np
import jax
import jax.numpy as jnp
from jax import lax
from jax.experimental import pallas as pl
from jax.experimental.pallas import tpu as pltpu

F32 = jnp.float32
BF16 = jnp.bfloat16

D_RNN = 512
RNN_BLOCKS = 8
CONV_WIDTH = 4
LRU_C = 8.0
N_HEADS = 8
N_KV = 2
HPG = N_HEADS // N_KV
HEAD_DIM = 64
D_ATTN = N_HEADS * HEAD_DIM
KV_W = N_KV * HEAD_DIM
CMP_LEN = 32
CMP_STRIDE = 16
SLC_BLK = 64
SLC_TOPN = 16
WIN = 512
FORCE_SCORE = 1.0e4
ROPE_THETA = 500000.0
ROPE_DIM = HEAD_DIM // 4
ROPE_HALF = ROPE_DIM // 2
EPS = 1e-6
ATTN_SCALE = HEAD_DIM ** -0.5
N_GATES = 3 * N_HEADS

LANES = 128
VMEM_LIMIT = 56 * 1024 * 1024

IN_TILE = 512
RNN_TILE = 256
ATT_TILE = 256
FFN_TILE = 512

_DN_T = (((1,), (1,)), ((), ()))


def _params(n_axes):
    return pltpu.CompilerParams(dimension_semantics=("arbitrary",) * n_axes,
                                vmem_limit_bytes=VMEM_LIMIT)


def _full(shape):
    return pl.BlockSpec(shape, lambda *_: (0,) * len(shape))


def _rope_table_kernel(pos_ref, inv_ref, cos_ref, sin_ref):
    ang = pos_ref[0].astype(F32) * inv_ref[...]
    cos_ref[0] = jnp.cos(ang)
    sin_ref[0] = jnp.sin(ang)


def _rope_tables(positions):
    B, T = positions.shape
    inv = jnp.power(ROPE_THETA, -jnp.arange(ROPE_HALF, dtype=F32) * 2.0 / ROPE_DIM).reshape(ROPE_HALF, 1)
    out = jax.ShapeDtypeStruct((B, ROPE_HALF, T), F32)
    return pl.pallas_call(
        _rope_table_kernel,
        grid=(B,),
        in_specs=[pl.BlockSpec((1, 1, T), lambda b: (b, 0, 0)), _full((ROPE_HALF, 1))],
        out_specs=[pl.BlockSpec((1, ROPE_HALF, T), lambda b: (b, 0, 0))] * 2,
        out_shape=[out, out],
        compiler_params=_params(1),
        name="rope_tables",
    )(positions.reshape(B, 1, T), inv)


def _norm_rope_t(blk, w_col, cos, sin):
    y = blk * lax.rsqrt(jnp.mean(blk * blk, axis=0, keepdims=True) + EPS) * w_col
    x1 = y[0:ROPE_HALF]
    x2 = y[ROPE_HALF:ROPE_DIM]
    return jnp.concatenate([x1 * cos - x2 * sin, x2 * cos + x1 * sin, y[ROPE_DIM:]], axis=0)


def _in_proj_kernel(x_ref, nw_ref, wn_ref, wt_ref, qw_ref, kw_ref, cos_ref, sin_ref,
                    xr_ref, gr_ref, kcvc_ref, vs_ref, vw_ref, gate_ref, q_ref, ks_ref, kwin_ref):
    x = x_ref[0]
    h = x * lax.rsqrt(jnp.mean(x * x, axis=-1, keepdims=True) + EPS) * nw_ref[...]
    hb = h.astype(BF16)
    pn = jnp.dot(hb, wn_ref[...], preferred_element_type=F32)
    pt = lax.dot_general(wt_ref[...], hb, _DN_T, preferred_element_type=F32)
    xr_ref[0] = pn[:, 0:512]
    gr_ref[0] = pn[:, 512:1024]
    kcvc_ref[0] = pn[:, 1024:1280]
    for g in range(N_KV):
        vs_ref[0, g] = pn[:, 1280 + g * LANES:1280 + (g + 1) * LANES].astype(BF16)
        vw_ref[0, g] = pn[:, 1536 + g * LANES:1536 + (g + 1) * LANES].astype(BF16)
    gate_ref[0] = jax.nn.sigmoid(pn[:, 1792:1920])
    cos = cos_ref[0]
    sin = sin_ref[0]
    qw = qw_ref[...]
    kw = kw_ref[...]
    for p in range(N_HEADS // 2):
        hs = [_norm_rope_t(pt[(2 * p + i) * HEAD_DIM:(2 * p + i + 1) * HEAD_DIM], qw, cos, sin) * ATTN_SCALE
              for i in range(2)]
        q_ref[0, p] = jnp.concatenate(hs, axis=0).T.astype(BF16)
    for g in range(N_KV):
        ks = _norm_rope_t(pt[D_ATTN + g * HEAD_DIM:D_ATTN + (g + 1) * HEAD_DIM], kw, cos, sin)
        ks_ref[0, g] = jnp.concatenate([ks, ks], axis=0).T.astype(BF16)
        kwn = _norm_rope_t(pt[D_ATTN + KV_W + g * HEAD_DIM:D_ATTN + KV_W + (g + 1) * HEAD_DIM], kw, cos, sin)
        kwin_ref[0, g] = jnp.concatenate([kwn, kwn], axis=0).T.astype(BF16)


def _in_proj(x, attn_norm_w, w_in, q_norm_w, k_norm_w, cos_t, sin_t):
    B, T, D = x.shape
    tm = min(IN_TILE, T)
    sp = np.cumsum((D_RNN, D_RNN, D_ATTN, KV_W, KV_W, KV_W, KV_W, KV_W, KV_W, N_GATES))
    xr_w, gr_w, q_w, kc_w, vc_w, ks_w, vs_w, kw_w, vw_w, gl_w = jnp.split(w_in, sp[:-1], axis=1)

    def dup(w):
        return jnp.concatenate([w[:, g * HEAD_DIM:(g + 1) * HEAD_DIM] for g in range(N_KV) for _ in range(2)], axis=1)

    wn = jnp.concatenate([xr_w, gr_w, kc_w, vc_w, dup(vs_w), dup(vw_w), gl_w,
                          jnp.zeros((D, LANES - N_GATES), w_in.dtype)], axis=1).astype(BF16)
    wt = jnp.concatenate([q_w, ks_w, kw_w], axis=1).T.astype(BF16)
    n_wn = wn.shape[1]
    n_wt = wt.shape[0]
    f32o = lambda *s: jax.ShapeDtypeStruct(s, F32)
    bfo = lambda *s: jax.ShapeDtypeStruct(s, BF16)
    tok = lambda w: pl.BlockSpec((1, tm, w), lambda b, i: (b, i, 0))
    grp = lambda n: pl.BlockSpec((1, n, tm, LANES), lambda b, i: (b, 0, i, 0))
    rope = pl.BlockSpec((1, ROPE_HALF, tm), lambda b, i: (b, 0, i))
    return pl.pallas_call(
        _in_proj_kernel,
        grid=(B, T // tm),
        in_specs=[tok(D), _full((1, D)), _full((D, n_wn)), _full((n_wt, D)),
                  _full((HEAD_DIM, 1)), _full((HEAD_DIM, 1)), rope, rope],
        out_specs=[tok(D_RNN), tok(D_RNN), tok(2 * KV_W), grp(N_KV), grp(N_KV), tok(LANES),
                   grp(N_HEADS // 2), grp(N_KV), grp(N_KV)],
        out_shape=[f32o(B, T, D_RNN), f32o(B, T, D_RNN), f32o(B, T, 2 * KV_W),
                   bfo(B, N_KV, T, LANES), bfo(B, N_KV, T, LANES), f32o(B, T, LANES),
                   bfo(B, N_HEADS // 2, T, LANES), bfo(B, N_KV, T, LANES), bfo(B, N_KV, T, LANES)],
        compiler_params=_params(2),
        name="in_proj",
    )(x, attn_norm_w.reshape(1, D), wn, wt, q_norm_w.reshape(HEAD_DIM, 1), k_norm_w.reshape(HEAD_DIM, 1),
      cos_t, sin_t)


def _rglru_kernel(xr_ref, gr_ref, cw_ref, cb_ref, wa_ref, ba_ref, wx_ref, bx_ref, lam_ref, ow_ref,
                  y_ref, ext_ref, h_ref):
    tt = xr_ref.shape[1]
    pad = 8

    @pl.when(pl.program_id(1) == 0)
    def _():
        ext_ref[0:pad, :] = jnp.zeros((pad, D_RNN), F32)
        h_ref[...] = jnp.zeros((1, D_RNN), F32)

    xr = xr_ref[0]
    ext_ref[pad:pad + tt, :] = xr
    xc = cb_ref[...]
    for k in range(CONV_WIDTH):
        off = pad - (CONV_WIDTH - 1) + k
        xc = xc + cw_ref[k:k + 1, :] * ext_ref[off:off + tt, :]
    ext_ref[0:pad, :] = xr[tt - pad:tt]
    xcb = xc.astype(BF16)
    r = jax.nn.sigmoid(jnp.dot(xcb, wa_ref[...], preferred_element_type=F32) + ba_ref[...])
    i = jax.nn.sigmoid(jnp.dot(xcb, wx_ref[...], preferred_element_type=F32) + bx_ref[...])
    z = -lam_ref[...]
    softplus = jnp.maximum(z, 0.0) + jnp.log1p(jnp.exp(-jnp.abs(z)))
    log_a = -LRU_C * r * softplus
    a = jnp.exp(log_a)
    t2 = 2.0 * log_a
    u = jnp.exp(t2)
    neg_expm1 = jnp.where(u == 1.0, -t2, (1.0 - u) * t2 / jnp.log(u))
    b = jnp.sqrt(neg_expm1) * (i * xc)
    row = lax.broadcasted_iota(jnp.int32, (tt, D_RNN), 0) & 7
    for s in (1, 2, 4):
        keep = row >= s
        a_sh = jnp.where(keep, pltpu.roll(a, s, 0), 1.0)
        b_sh = jnp.where(keep, pltpu.roll(b, s, 0), 0.0)
        b = a * b_sh + b
        a = a * a_sh
    h = h_ref[...]
    hs = []
    for g in range(tt // 8):
        hg = a[g * 8:(g + 1) * 8] * h + b[g * 8:(g + 1) * 8]
        h = hg[7:8]
        hs.append(hg)
    h_ref[...] = h
    y = jax.nn.gelu(gr_ref[0]) * jnp.concatenate(hs, axis=0)
    y = y * lax.rsqrt(jnp.mean(y * y, axis=-1, keepdims=True) + EPS) * ow_ref[...]
    y_ref[0] = y.astype(BF16)


def _block_diag(w):
    n, d, _ = w.shape
    eye = jnp.eye(n, dtype=w.dtype)
    return (eye[:, None, :, None] * w[:, :, None, :]).reshape(n * d, n * d)


def _rglru(xr, gr, conv_w, conv_b, gate_a_w, gate_a_b, gate_x_w, gate_x_b, lru_lambda, out_norm_w):
    B, T, _ = xr.shape
    tt = min(RNN_TILE, T)
    tok = pl.BlockSpec((1, tt, D_RNN), lambda b, i: (b, i, 0))
    row = lambda a: a.reshape(1, D_RNN)
    return pl.pallas_call(
        _rglru_kernel,
        grid=(B, T // tt),
        in_specs=[tok, tok, _full((CONV_WIDTH, D_RNN)), _full((1, D_RNN)),
                  _full((D_RNN, D_RNN)), _full((1, D_RNN)), _full((D_RNN, D_RNN)), _full((1, D_RNN)),
                  _full((1, D_RNN)), _full((1, D_RNN))],
        out_specs=tok,
        out_shape=jax.ShapeDtypeStruct((B, T, D_RNN), BF16),
        scratch_shapes=[pltpu.VMEM((tt + 8, D_RNN), F32), pltpu.VMEM((1, D_RNN), F32)],
        compiler_params=_params(2),
        name="rglru",
    )(xr, gr, conv_w, row(conv_b), _block_diag(gate_a_w).astype(BF16), row(gate_a_b),
      _block_diag(gate_x_w).astype(BF16), row(gate_x_b), row(lru_lambda), row(out_norm_w))


def _compress_kernel(rk_ref, rv_ref, pos_ref, w1k_ref, w2k_ref, w1v_ref, w2v_ref, kw_ref, cos_ref, sin_ref,
                     kc_ref, vc_ref):
    half = pos_ref.shape[1]
    n_rows = rk_ref.shape[2]

    def hidden(r, w1_ref):
        top = jnp.dot((r + pos_ref[0:1, :]).astype(BF16), w1_ref[0:half, :], preferred_element_type=F32)
        bot = jnp.dot((r + pos_ref[1:2, :]).astype(BF16), w1_ref[half:2 * half, :], preferred_element_type=F32)
        return jax.nn.gelu(top + pltpu.roll(bot, n_rows - 1, 0)).astype(BF16)

    kc_t = lax.dot_general(w2k_ref[...], hidden(rk_ref[0, 0], w1k_ref), _DN_T, preferred_element_type=F32)
    kc_t = _norm_rope_t(kc_t, kw_ref[...], cos_ref[0], sin_ref[0])
    kc_ref[0, 0] = jnp.concatenate([kc_t, kc_t], axis=0).T.astype(BF16)
    vc_ref[0, 0] = jnp.dot(hidden(rv_ref[0, 0], w1v_ref), w2v_ref[...], preferred_element_type=F32).astype(BF16)


def _compress(kcvc, cmp_pos, k_w1, k_w2, v_w1, v_w2, k_norm_w, cos_t, sin_t):
    B, T, _ = kcvc.shape
    n_rows = T // CMP_STRIDE
    n_cmp = (T - CMP_LEN) // CMP_STRIDE + 1
    half = CMP_STRIDE * HEAD_DIM
    hid = k_w1.shape[1]

    def rows(tokens):
        return tokens.reshape(B, T, N_KV, HEAD_DIM).transpose(0, 2, 1, 3).reshape(B, N_KV, n_rows, half)

    def at_block_end(tab):
        tab = tab[:, :, CMP_LEN - 1::CMP_STRIDE]
        return jnp.pad(tab, ((0, 0), (0, 0), (0, n_rows - n_cmp)))

    blk = pl.BlockSpec((1, 1, n_rows, half), lambda b, g: (b, g, 0, 0))
    rope = pl.BlockSpec((1, ROPE_HALF, n_rows), lambda b, g: (b, 0, 0))
    out = pl.BlockSpec((1, 1, n_rows, LANES), lambda b, g: (b, g, 0, 0))
    return pl.pallas_call(
        _compress_kernel,
        grid=(B, N_KV),
        in_specs=[blk, blk, _full((2, half)), _full((2 * half, hid)), _full((HEAD_DIM, hid)),
                  _full((2 * half, hid)), _full((hid, LANES)), _full((HEAD_DIM, 1)), rope, rope],
        out_specs=[out, out],
        out_shape=[jax.ShapeDtypeStruct((B, N_KV, n_rows, LANES), BF16)] * 2,
        compiler_params=_params(2),
        name="compress",
    )(rows(kcvc[:, :, :KV_W]), rows(kcvc[:, :, KV_W:]), cmp_pos.reshape(2, half),
      k_w1.astype(BF16), k_w2.T.astype(BF16), v_w1.astype(BF16),
      jnp.concatenate([v_w2, v_w2], axis=1).astype(BF16), k_norm_w.reshape(HEAD_DIM, 1),
      at_block_end(cos_t), at_block_end(sin_t))


def _attn_kernel(q_ref, kc_ref, vc_ref, ks_ref, vs_ref, kw_ref, vw_ref, gate_ref, cover_ref, exp_ref, ow_ref,
                 y_ref, acc_ref, m_ref, l_ref, *, n_cmp):
    tq = q_ref.shape[2]
    ck = exp_ref.shape[2]
    n_rows = kc_ref.shape[2]
    n_blk = SLC_BLK
    qi = pl.program_id(1)
    q0 = qi * tq
    neg_inf = -jnp.inf
    lo = lax.broadcasted_iota(jnp.int32, (tq, LANES), 1) < HEAD_DIM
    t_col = q0 + lax.broadcasted_iota(jnp.int32, (tq, 1), 0)
    gates = gate_ref[0]
    c_idx = lax.broadcasted_iota(jnp.int32, (tq, n_rows), 1)
    cmask = ((c_idx * CMP_STRIDE + (CMP_LEN - 1)) <= t_col) & (c_idx < n_cmp)
    t_row = q0 + lax.broadcasted_iota(jnp.int32, (n_blk, tq), 1)
    j_row = lax.broadcasted_iota(jnp.int32, (n_blk, tq), 0)
    cur = t_row // SLC_BLK
    forced = (j_row == 0) | (j_row == cur) | (j_row == cur - 1)
    valid = (j_row * SLC_BLK) <= t_row
    k_lane = lax.broadcasted_iota(jnp.int32, (tq, ck), 1)
    sub_row = lax.broadcasted_iota(jnp.int32, (8, tq), 0)

    def flash(q_heads, k_ref, v_ref, g, lo_chunk, hi_chunk, mask_fn):
        for h in range(HPG):
            m_ref[h] = jnp.full((tq, 1), neg_inf, F32)
            l_ref[h] = jnp.zeros((tq, 1), F32)
            acc_ref[h] = jnp.zeros((tq, LANES), F32)

        def body(ci, carry):
            rows = pl.ds(pl.multiple_of(ci * ck, ck), ck)
            k = k_ref[0, g, rows, :]
            v = v_ref[0, g, rows, :]
            msk = mask_fn(ci, ci * ck + k_lane)
            for h in range(HPG):
                s = lax.dot_general(q_heads[h], k, _DN_T, preferred_element_type=F32)
                s = jnp.where(msk, s, neg_inf)
                m_prev = m_ref[h]
                m_new = jnp.maximum(m_prev, jnp.max(s, axis=-1, keepdims=True))
                m_safe = jnp.where(m_new == neg_inf, 0.0, m_new)
                alpha = jnp.exp(m_prev - m_safe)
                p = jnp.exp(s - m_safe)
                l_ref[h] = alpha * l_ref[h] + jnp.sum(p, axis=-1, keepdims=True)
                acc_ref[h] = alpha * acc_ref[h] + jnp.dot(p.astype(BF16), v, preferred_element_type=F32)
                m_ref[h] = m_new
            return carry

        lax.fori_loop(lo_chunk, hi_chunk, body, 0)
        return [acc_ref[h] / jnp.maximum(l_ref[h], 1e-30) for h in range(HPG)]

    outs = []
    for g in range(N_KV):
        q_heads = []
        for p in (2 * g, 2 * g + 1):
            qp = q_ref[0, p]
            zero = jnp.zeros_like(qp)
            q_heads += [jnp.where(lo, qp, zero), jnp.where(lo, zero, qp)]

        kc = kc_ref[0, g]
        vc = vc_ref[0, g]
        imp = jnp.zeros((tq, LANES), F32)
        o_cmp = []
        for h in range(HPG):
            s = lax.dot_general(q_heads[h], kc, _DN_T, preferred_element_type=F32)
            s = jnp.where(cmask, s, neg_inf)
            m = jnp.max(s, axis=-1, keepdims=True)
            m = jnp.where(m == neg_inf, 0.0, m)
            e = jnp.where(cmask, jnp.exp(s - m), 0.0)
            pb = (e / jnp.maximum(jnp.sum(e, axis=-1, keepdims=True), 1e-30)).astype(BF16)
            o_cmp.append(jnp.dot(pb, vc, preferred_element_type=F32))
            imp = imp + jnp.dot(pb, cover_ref[...], preferred_element_type=F32)

        score = jnp.where(forced, FORCE_SCORE, jnp.where(valid, imp.T[0:n_blk], neg_inf))
        slabs = [score[8 * sj:8 * sj + 8] for sj in range(n_blk // 8)]
        ranks = [jnp.zeros((8, tq), F32) for _ in slabs]
        for jp in range(n_blk):
            other = jnp.broadcast_to(score[jp:jp + 1, :], (8, tq))
            for sj, slab in enumerate(slabs):
                if jp < 8 * sj:
                    inc = jnp.where(other >= slab, 1.0, 0.0)
                elif jp >= 8 * sj + 8:
                    inc = jnp.where(other > slab, 1.0, 0.0)
                else:
                    tie = jnp.where(sub_row > jp - 8 * sj, 1.0, 0.0)
                    inc = jnp.where(other > slab, 1.0, jnp.where(other == slab, tie, 0.0))
                ranks[sj] = ranks[sj] + inc
        rank = jnp.concatenate(ranks, axis=0)
        sel_t = jnp.where(rank < SLC_TOPN, 1.0, 0.0)
        sel = jnp.concatenate([sel_t, jnp.zeros((LANES - n_blk, tq), F32)], axis=0).T.astype(BF16)

        def sel_mask(ci, k_pos):
            picked = jnp.dot(sel, exp_ref[ci], preferred_element_type=F32) > 0.5
            return picked & (k_pos <= t_col)

        def win_mask(ci, k_pos):
            return (k_pos <= t_col) & ((t_col - k_pos) < WIN)

        o_slc = flash(q_heads, ks_ref, vs_ref, g, 0, qi + 1, sel_mask)
        o_win = flash(q_heads, kw_ref, vw_ref, g, jnp.maximum(qi - WIN // ck, 0), qi + 1, win_mask)

        def gated(h):
            c = (g * HPG + h) * 3
            return (gates[:, c:c + 1] * o_cmp[h] + gates[:, c + 1:c + 2] * o_slc[h]
                    + gates[:, c + 2:c + 3] * o_win[h])

        for pair in range(HPG // 2):
            outs.append(jnp.where(lo, gated(2 * pair), gated(2 * pair + 1)))

    y = jnp.concatenate(outs, axis=1)
    y = y * lax.rsqrt(jnp.mean(y * y, axis=-1, keepdims=True) + EPS) * ow_ref[...]
    y_ref[0] = y.astype(BF16)


def _attention(q, kc, vc, ks, vs, kw, vw, gates, out_norm_w):
    B, _, T, _ = q.shape
    tq = min(ATT_TILE, T)
    ck = tq
    n_rows = kc.shape[2]
    n_cmp = (T - CMP_LEN) // CMP_STRIDE + 1
    nb = T // SLC_BLK
    assert nb <= SLC_BLK and WIN % ck == 0 and T % tq == 0
    cs = np.arange(n_cmp)[:, None] * CMP_STRIDE
    bs = np.arange(nb)[None, :] * SLC_BLK
    cover = np.clip(np.minimum(cs + CMP_LEN, bs + SLC_BLK) - np.maximum(cs, bs), 0, None) / CMP_LEN
    cover = jnp.asarray(np.pad(cover, ((0, n_rows - n_cmp), (0, LANES - nb))), BF16)
    key_blk = (np.arange(T) // SLC_BLK).reshape(T // ck, 1, ck)
    expand = jnp.asarray(key_blk == np.arange(LANES).reshape(1, LANES, 1), BF16)

    per_b = lambda n, r: pl.BlockSpec((1, n, r, LANES), lambda b, i: (b, 0, 0, 0))
    return pl.pallas_call(
        functools.partial(_attn_kernel, n_cmp=n_cmp),
        grid=(B, T // tq),
        in_specs=[pl.BlockSpec((1, N_HEADS // 2, tq, LANES), lambda b, i: (b, 0, i, 0)),
                  per_b(N_KV, n_rows), per_b(N_KV, n_rows),
                  per_b(N_KV, T), per_b(N_KV, T), per_b(N_KV, T), per_b(N_KV, T),
                  pl.BlockSpec((1, tq, LANES), lambda b, i: (b, i, 0)),
                  _full((n_rows, LANES)), _full((T // ck, LANES, ck)), _full((1, D_ATTN))],
        out_specs=pl.BlockSpec((1, tq, D_ATTN), lambda b, i: (b, i, 0)),
        out_shape=jax.ShapeDtypeStruct((B, T, D_ATTN), BF16),
        scratch_shapes=[pltpu.VMEM((HPG, tq, LANES), F32), pltpu.VMEM((HPG, tq, 1), F32),
                        pltpu.VMEM((HPG, tq, 1), F32)],
        compiler_params=_params(2),
        name="nsa_attention",
    )(q, kc, vc, ks, vs, kw, vw, gates, cover, expand, out_norm_w.reshape(1, D_ATTN))


def _out_ffn_kernel(x_ref, yr_ref, ya_ref, wo_r_ref, wo_a_ref, fw_ref, wg_ref, wu_ref, wd_ref, o_ref):
    x1 = (x_ref[...] + jnp.dot(yr_ref[...], wo_r_ref[...], preferred_element_type=F32)
          + jnp.dot(ya_ref[...], wo_a_ref[...], preferred_element_type=F32))
    h = (x1 * lax.rsqrt(jnp.mean(x1 * x1, axis=-1, keepdims=True) + EPS) * fw_ref[...]).astype(BF16)
    gate = jnp.dot(h, wg_ref[...], preferred_element_type=F32)
    up = jnp.dot(h, wu_ref[...], preferred_element_type=F32)
    act = (gate * jax.nn.sigmoid(gate) * up).astype(BF16)
    o_ref[...] = x1 + jnp.dot(act, wd_ref[...], preferred_element_type=F32)


def _out_ffn(x, y_rnn, y_att, w_out, ffn_norm_w, w_gate, w_up, w_down):
    B, T, D = x.shape
    n = B * T
    tm = min(FFN_TILE, n)
    d_ff = w_gate.shape[1]
    tok = lambda w: pl.BlockSpec((tm, w), lambda i: (i, 0))
    once = lambda shape: pl.BlockSpec(shape, lambda i: (0, 0), pipeline_mode=pl.Buffered(1))
    out = pl.pallas_call(
        _out_ffn_kernel,
        grid=(n // tm,),
        in_specs=[tok(D), tok(D_RNN), tok(D_ATTN), once((D_RNN, D)), once((D_ATTN, D)), once((1, D)),
                  once((D, d_ff)), once((D, d_ff)), once((d_ff, D))],
        out_specs=tok(D),
        out_shape=jax.ShapeDtypeStruct((n, D), F32),
        compiler_params=_params(1),
        name="out_ffn",
    )(x.reshape(n, D), y_rnn.reshape(n, D_RNN), y_att.reshape(n, D_ATTN),
      w_out[:D_RNN].astype(BF16), w_out[D_RNN:].astype(BF16), ffn_norm_w.reshape(1, D),
      w_gate.astype(BF16), w_up.astype(BF16), w_down.astype(BF16))
    return out.reshape(B, T, D)


def kernel(x, positions, attn_norm_w, w_in, conv_w, conv_b, gate_a_w, gate_a_b, gate_x_w, gate_x_b, lru_lambda, q_norm_w, k_norm_w, cmp_pos, cmp_k_w1, cmp_k_w2, cmp_v_w1, cmp_v_w2, rnn_out_norm_w, attn_out_norm_w, w_out, ffn_norm_w, w_gate, w_up, w_down):
    cos_t, sin_t = _rope_tables(positions)
    for l in range(attn_norm_w.shape[0]):
        xr, gr, kcvc, vs, vw, gates, q, ks, kw = _in_proj(x, attn_norm_w[l], w_in[l], q_norm_w[l], k_norm_w[l],
                                                          cos_t, sin_t)
        y_rnn = _rglru(xr, gr, conv_w[l], conv_b[l], gate_a_w[l], gate_a_b[l], gate_x_w[l], gate_x_b[l],
                       lru_lambda[l], rnn_out_norm_w[l])
        kc, vc = _compress(kcvc, cmp_pos[l], cmp_k_w1[l], cmp_k_w2[l], cmp_v_w1[l], cmp_v_w2[l], k_norm_w[l],
                           cos_t, sin_t)
        y_att = _attention(q, kc, vc, ks, vs, kw, vw, gates, attn_out_norm_w[l])
        x = _out_ffn(x, y_rnn, y_att, w_out[l], ffn_norm_w[l], w_gate[l], w_up[l], w_down[l])
    return x
```

```python
import functools

import numpy as np
import jax
import jax.numpy as jnp
from jax import lax
from jax.experimental import pallas as pl
from jax.experimental.pallas import tpu as pltpu

F32 = jnp.float32
BF16 = jnp.bfloat16

D_RNN = 512
RNN_BLOCKS = 8
CONV_WIDTH = 4
LRU_C = 8.0
N_HEADS = 8
N_KV = 2
HPG = N_HEADS // N_KV
HEAD_DIM = 64
D_ATTN = N_HEADS * HEAD_DIM
KV_W = N_KV * HEAD_DIM
CMP_LEN = 32
CMP_STRIDE = 16
SLC_BLK = 64
SLC_TOPN = 16
WIN = 512
FORCE_SCORE = 1.0e4
ROPE_THETA = 500000.0
ROPE_DIM = HEAD_DIM // 4
ROPE_HALF = ROPE_DIM // 2
EPS = 1e-6
ATTN_SCALE = HEAD_DIM ** -0.5
N_GATES = 3 * N_HEADS
GATE_ROWS = 32

LANES = 128
VMEM_LIMIT = 56 * 1024 * 1024

IN_TILE = 512
RNN_TILE = 256
ATT_TILE = 256
FFN_TILE = 512

_DN_T = (((1,), (1,)), ((), ()))


def _params(n_axes):
    return pltpu.CompilerParams(dimension_semantics=("arbitrary",) * n_axes,
                                vmem_limit_bytes=VMEM_LIMIT)


def _full(shape):
    return pl.BlockSpec(shape, lambda *_: (0,) * len(shape))


def _rope_table_kernel(pos_ref, inv_ref, cos_ref, sin_ref):
    ang = pos_ref[0].astype(F32) * inv_ref[...]
    cos_ref[0] = jnp.cos(ang)
    sin_ref[0] = jnp.sin(ang)


def _rope_tables(positions):
    B, T = positions.shape
    inv = jnp.power(ROPE_THETA, -jnp.arange(ROPE_HALF, dtype=F32) * 2.0 / ROPE_DIM).reshape(ROPE_HALF, 1)
    out = jax.ShapeDtypeStruct((B, ROPE_HALF, T), F32)
    return pl.pallas_call(
        _rope_table_kernel,
        grid=(B,),
        in_specs=[pl.BlockSpec((1, 1, T), lambda b: (b, 0, 0)), _full((ROPE_HALF, 1))],
        out_specs=[pl.BlockSpec((1, ROPE_HALF, T), lambda b: (b, 0, 0))] * 2,
        out_shape=[out, out],
        compiler_params=_params(1),
        name="rope_tables",
    )(positions.reshape(B, 1, T), inv)


def _norm_rope_t(blk, w_col, cos, sin):
    y = blk * lax.rsqrt(jnp.mean(blk * blk, axis=0, keepdims=True) + EPS) * w_col
    x1 = y[0:ROPE_HALF]
    x2 = y[ROPE_HALF:ROPE_DIM]
    return jnp.concatenate([x1 * cos - x2 * sin, x2 * cos + x1 * sin, y[ROPE_DIM:]], axis=0)


def _in_proj_kernel(x_ref, nw_ref, wn_ref, wt_ref, qw_ref, kw_ref, cos_ref, sin_ref,
                    xr_ref, gr_ref, kcvc_ref, q_ref, ks_ref, kwin_ref, vs_ref, vw_ref, gate_ref):
    ck = vs_ref.shape[3]
    tm = x_ref.shape[1]
    x = x_ref[0]
    h = x * lax.rsqrt(jnp.mean(x * x, axis=-1, keepdims=True) + EPS) * nw_ref[...]
    hb = h.astype(BF16)
    pn = jnp.dot(hb, wn_ref[...], preferred_element_type=F32)
    pt = lax.dot_general(wt_ref[...], hb, _DN_T, preferred_element_type=F32)
    xr_ref[0] = pn[:, 0:D_RNN]
    gr_ref[0] = pn[:, D_RNN:2 * D_RNN]
    kcvc_ref[0] = pn[:, 2 * D_RNN:2 * D_RNN + 2 * KV_W]
    cos = cos_ref[0]
    sin = sin_ref[0]
    qw = qw_ref[...]
    kw = kw_ref[...]
    for hd in range(N_HEADS):
        rows = slice(hd * HEAD_DIM, (hd + 1) * HEAD_DIM)
        q_ref[0, rows, :] = (_norm_rope_t(pt[rows], qw, cos, sin) * ATTN_SCALE).astype(BF16)
    for base, k_ref in ((D_ATTN, ks_ref), (D_ATTN + KV_W, kwin_ref)):
        k_t = jnp.concatenate([_norm_rope_t(pt[base + g * HEAD_DIM:base + (g + 1) * HEAD_DIM], kw, cos, sin)
                               for g in range(N_KV)], axis=0)
        k_ref[0] = k_t.T.astype(BF16)
    for base, v_ref in ((D_ATTN + 2 * KV_W, vs_ref), (D_ATTN + 3 * KV_W, vw_ref)):
        for c in range(tm // ck):
            v_ref[0, c] = pt[base:base + KV_W, c * ck:(c + 1) * ck].astype(BF16)
    gbase = D_ATTN + 4 * KV_W
    gate_ref[0] = jax.nn.sigmoid(pt[gbase:gbase + GATE_ROWS])


def _in_proj(x, attn_norm_w, w_in, q_norm_w, k_norm_w, cos_t, sin_t):
    B, T, D = x.shape
    tm = min(IN_TILE, T)
    ck = min(ATT_TILE, T)
    sp = np.cumsum((D_RNN, D_RNN, D_ATTN, KV_W, KV_W, KV_W, KV_W, KV_W, KV_W, N_GATES))
    xr_w, gr_w, q_w, kc_w, vc_w, ks_w, vs_w, kw_w, vw_w, gl_w = jnp.split(w_in, sp[:-1], axis=1)
    wn = jnp.concatenate([xr_w, gr_w, kc_w, vc_w], axis=1).astype(BF16)
    wt = jnp.concatenate([q_w, ks_w, kw_w, vs_w, vw_w, gl_w,
                          jnp.zeros((D, GATE_ROWS - N_GATES), w_in.dtype)], axis=1).T.astype(BF16)
    n_wn = wn.shape[1]
    n_wt = wt.shape[0]
    f32o = lambda *s: jax.ShapeDtypeStruct(s, F32)
    bfo = lambda *s: jax.ShapeDtypeStruct(s, BF16)
    tok = lambda w: pl.BlockSpec((1, tm, w), lambda b, i: (b, i, 0))
    feat = lambda r: pl.BlockSpec((1, r, tm), lambda b, i: (b, 0, i))
    vchunk = pl.BlockSpec((1, tm // ck, KV_W, ck), lambda b, i: (b, i, 0, 0))
    return pl.pallas_call(
        _in_proj_kernel,
        grid=(B, T // tm),
        in_specs=[tok(D), _full((1, D)), _full((D, n_wn)), _full((n_wt, D)),
                  _full((HEAD_DIM, 1)), _full((HEAD_DIM, 1)), feat(ROPE_HALF), feat(ROPE_HALF)],
        out_specs=[tok(D_RNN), tok(D_RNN), tok(2 * KV_W), feat(D_ATTN), tok(KV_W), tok(KV_W),
                   vchunk, vchunk, feat(GATE_ROWS)],
        out_shape=[f32o(B, T, D_RNN), f32o(B, T, D_RNN), f32o(B, T, 2 * KV_W),
                   bfo(B, D_ATTN, T), bfo(B, T, KV_W), bfo(B, T, KV_W),
                   bfo(B, T // ck, KV_W, ck), bfo(B, T // ck, KV_W, ck), f32o(B, GATE_ROWS, T)],
        compiler_params=_params(2),
        name="in_proj",
    )(x, attn_norm_w.reshape(1, D), wn, wt, q_norm_w.reshape(HEAD_DIM, 1), k_norm_w.reshape(HEAD_DIM, 1),
      cos_t, sin_t)


def _rglru_kernel(xr_ref, gr_ref, cw_ref, cb_ref, wa_ref, ba_ref, wx_ref, bx_ref, lam_ref, ow_ref,
                  y_ref, ext_ref, h_ref):
    tt = xr_ref.shape[1]
    pad = 8

    @pl.when(pl.program_id(1) == 0)
    def _():
        ext_ref[0:pad, :] = jnp.zeros((pad, D_RNN), F32)
        h_ref[...] = jnp.zeros((1, D_RNN), F32)

    xr = xr_ref[0]
    ext_ref[pad:pad + tt, :] = xr
    xc = cb_ref[...]
    for k in range(CONV_WIDTH):
        off = pad - (CONV_WIDTH - 1) + k
        xc = xc + cw_ref[k:k + 1, :] * ext_ref[off:off + tt, :]
    ext_ref[0:pad, :] = xr[tt - pad:tt]
    xcb = xc.astype(BF16)
    r = jax.nn.sigmoid(jnp.dot(xcb, wa_ref[...], preferred_element_type=F32) + ba_ref[...])
    i = jax.nn.sigmoid(jnp.dot(xcb, wx_ref[...], preferred_element_type=F32) + bx_ref[...])
    z = -lam_ref[...]
    softplus = jnp.maximum(z, 0.0) + jnp.log1p(jnp.exp(-jnp.abs(z)))
    log_a = -LRU_C * r * softplus
    a = jnp.exp(log_a)
    t2 = 2.0 * log_a
    u = jnp.exp(t2)
    neg_expm1 = jnp.where(u == 1.0, -t2, (1.0 - u) * t2 / jnp.log(u))
    b = jnp.sqrt(neg_expm1) * (i * xc)
    row = lax.broadcasted_iota(jnp.int32, (tt, D_RNN), 0) & 7
    for s in (1, 2, 4):
        keep = row >= s
        a_sh = jnp.where(keep, pltpu.roll(a, s, 0), 1.0)
        b_sh = jnp.where(keep, pltpu.roll(b, s, 0), 0.0)
        b = a * b_sh + b
        a = a * a_sh
    h = h_ref[...]
    hs = []
    for g in range(tt // 8):
        hg = a[g * 8:(g + 1) * 8] * h + b[g * 8:(g + 1) * 8]
        h = hg[7:8]
        hs.append(hg)
    h_ref[...] = h
    y = jax.nn.gelu(gr_ref[0]) * jnp.concatenate(hs, axis=0)
    y = y * lax.rsqrt(jnp.mean(y * y, axis=-1, keepdims=True) + EPS) * ow_ref[...]
    y_ref[0] = y.astype(BF16)


def _block_diag(w):
    n, d, _ = w.shape
    eye = jnp.eye(n, dtype=w.dtype)
    return (eye[:, None, :, None] * w[:, :, None, :]).reshape(n * d, n * d)


def _rglru(xr, gr, conv_w, conv_b, gate_a_w, gate_a_b, gate_x_w, gate_x_b, lru_lambda, out_norm_w):
    B, T, _ = xr.shape
    tt = min(RNN_TILE, T)
    tok = pl.BlockSpec((1, tt, D_RNN), lambda b, i: (b, i, 0))
    row = lambda a: a.reshape(1, D_RNN)
    return pl.pallas_call(
        _rglru_kernel,
        grid=(B, T // tt),
        in_specs=[tok, tok, _full((CONV_WIDTH, D_RNN)), _full((1, D_RNN)),
                  _full((D_RNN, D_RNN)), _full((1, D_RNN)), _full((D_RNN, D_RNN)), _full((1, D_RNN)),
                  _full((1, D_RNN)), _full((1, D_RNN))],
        out_specs=tok,
        out_shape=jax.ShapeDtypeStruct((B, T, D_RNN), BF16),
        scratch_shapes=[pltpu.VMEM((tt + 8, D_RNN), F32), pltpu.VMEM((1, D_RNN), F32)],
        compiler_params=_params(2),
        name="rglru",
    )(xr, gr, conv_w, row(conv_b), _block_diag(gate_a_w).astype(BF16), row(gate_a_b),
      _block_diag(gate_x_w).astype(BF16), row(gate_x_b), row(lru_lambda), row(out_norm_w))


def _compress_kernel(rk_ref, rv_ref, pos_ref, w1k_ref, w2k_ref, w1v_ref, w2v_ref, kw_ref, cos_ref, sin_ref,
                     kc_ref, vc_ref):
    half = pos_ref.shape[1]
    n_rows = rk_ref.shape[2]

    def hidden(r, w1_ref):
        top = jnp.dot((r + pos_ref[0:1, :]).astype(BF16), w1_ref[0:half, :], preferred_element_type=F32)
        bot = jnp.dot((r + pos_ref[1:2, :]).astype(BF16), w1_ref[half:2 * half, :], preferred_element_type=F32)
        return jax.nn.gelu(top + pltpu.roll(bot, n_rows - 1, 0)).astype(BF16)

    kc_t = []
    vc_t = []
    for g in range(N_KV):
        k_t = lax.dot_general(w2k_ref[...], hidden(rk_ref[0, g], w1k_ref), _DN_T, preferred_element_type=F32)
        kc_t.append(_norm_rope_t(k_t, kw_ref[...], cos_ref[0], sin_ref[0]))
        vc_t.append(lax.dot_general(w2v_ref[...], hidden(rv_ref[0, g], w1v_ref), _DN_T,
                                    preferred_element_type=F32))
    kc_ref[0] = jnp.concatenate(kc_t, axis=0).T.astype(BF16)
    vc_ref[0] = jnp.concatenate(vc_t, axis=0).astype(BF16)


def _compress(kcvc, cmp_pos, k_w1, k_w2, v_w1, v_w2, k_norm_w, cos_t, sin_t):
    B, T, _ = kcvc.shape
    n_rows = T // CMP_STRIDE
    n_cmp = (T - CMP_LEN) // CMP_STRIDE + 1
    half = CMP_STRIDE * HEAD_DIM
    hid = k_w1.shape[1]

    def rows(tokens):
        return tokens.reshape(B, T, N_KV, HEAD_DIM).transpose(0, 2, 1, 3).reshape(B, N_KV, n_rows, half)

    def at_block_end(tab):
        tab = tab[:, :, CMP_LEN - 1::CMP_STRIDE]
        return jnp.pad(tab, ((0, 0), (0, 0), (0, n_rows - n_cmp)))

    blk = pl.BlockSpec((1, N_KV, n_rows, half), lambda b: (b, 0, 0, 0))
    rope = pl.BlockSpec((1, ROPE_HALF, n_rows), lambda b: (b, 0, 0))
    return pl.pallas_call(
        _compress_kernel,
        grid=(B,),
        in_specs=[blk, blk, _full((2, half)), _full((2 * half, hid)), _full((HEAD_DIM, hid)),
                  _full((2 * half, hid)), _full((HEAD_DIM, hid)), _full((HEAD_DIM, 1)), rope, rope],
        out_specs=[pl.BlockSpec((1, n_rows, KV_W), lambda b: (b, 0, 0)),
                   pl.BlockSpec((1, KV_W, n_rows), lambda b: (b, 0, 0))],
        out_shape=[jax.ShapeDtypeStruct((B, n_rows, KV_W), BF16), jax.ShapeDtypeStruct((B, KV_W, n_rows), BF16)],
        compiler_params=_params(1),
        name="compress",
    )(rows(kcvc[:, :, :KV_W]), rows(kcvc[:, :, KV_W:]), cmp_pos.reshape(2, half),
      k_w1.astype(BF16), k_w2.T.astype(BF16), v_w1.astype(BF16), v_w2.T.astype(BF16),
      k_norm_w.reshape(HEAD_DIM, 1), at_block_end(cos_t), at_block_end(sin_t))


def _attn_kernel(q_ref, kc_ref, vc_ref, ks_ref, vs_ref, kw_ref, vw_ref, gate_ref, cover_ref, ow_ref,
                 y_ref, qg_ref, acc_ref, m_ref, bias_ref, *, n_cmp):
    tq = q_ref.shape[2]
    ck = vs_ref.shape[3]
    n_rows = kc_ref.shape[1]
    n_blk = SLC_BLK
    bpc = ck // SLC_BLK
    l_rows = 16
    qi = pl.program_id(1)
    q0 = qi * tq
    neg_inf = -jnp.inf
    t_row = q0 + lax.broadcasted_iota(jnp.int32, (1, tq), 1)
    ones_rows = jnp.ones((l_rows, ck), BF16)
    tile_h = lambda a: jnp.concatenate([a] * HPG, axis=1)
    head = lambda a, h: a[:, h * tq:(h + 1) * tq]

    zero_half = jnp.zeros((HEAD_DIM, HPG * tq), BF16)
    for g in range(N_KV):
        qg = jnp.concatenate([q_ref[0, (g * HPG + h) * HEAD_DIM:(g * HPG + h + 1) * HEAD_DIM, :]
                              for h in range(HPG)], axis=1)
        qg_ref[g] = jnp.concatenate([qg, zero_half] if g == 0 else [zero_half, qg], axis=0)

    c_col = lax.broadcasted_iota(jnp.int32, (n_rows, tq), 0)
    cmask = tile_h(((c_col * CMP_STRIDE + (CMP_LEN - 1)) <= t_row) & (c_col < n_cmp))
    j_col = lax.broadcasted_iota(jnp.int32, (n_blk, tq), 0)
    cur = t_row // SLC_BLK
    forced = (j_col == 0) | (j_col == cur) | (j_col == cur - 1)
    valid = (j_col * SLC_BLK) <= t_row
    sub_row = lax.broadcasted_iota(jnp.int32, (8, tq), 0)
    kc = kc_ref[0]
    o_cmp = []
    for g in range(N_KV):
        v_cov = jnp.concatenate([vc_ref[0, g * HEAD_DIM:(g + 1) * HEAD_DIM, :], cover_ref[...]], axis=0)
        s = jnp.dot(kc, qg_ref[g], preferred_element_type=F32)
        s = jnp.where(cmask, s, neg_inf)
        m = jnp.max(s, axis=0, keepdims=True)
        m = jnp.where(m == neg_inf, 0.0, m)
        e = jnp.where(cmask, jnp.exp(s - m), 0.0)
        pb = (e / jnp.maximum(jnp.sum(e, axis=0, keepdims=True), 1e-30)).astype(BF16)
        r = jnp.dot(v_cov, pb, preferred_element_type=F32)
        o_cmp += [head(r[0:HEAD_DIM], h) for h in range(HPG)]
        imp = head(r[HEAD_DIM:], 0)
        for h in range(1, HPG):
            imp = imp + head(r[HEAD_DIM:], h)
        score = jnp.where(forced, FORCE_SCORE, jnp.where(valid, imp, neg_inf))
        slabs = [score[8 * sj:8 * sj + 8] for sj in range(n_blk // 8)]
        ranks = [jnp.zeros((8, tq), F32) for _ in slabs]
        for jp in range(n_blk):
            other = jnp.broadcast_to(score[jp:jp + 1, :], (8, tq))
            for sj, slab in enumerate(slabs):
                if jp < 8 * sj:
                    inc = jnp.where(other >= slab, 1.0, 0.0)
                elif jp >= 8 * sj + 8:
                    inc = jnp.where(other > slab, 1.0, 0.0)
                else:
                    tie = jnp.where(sub_row > jp - 8 * sj, 1.0, 0.0)
                    inc = jnp.where(other > slab, 1.0, jnp.where(other == slab, tie, 0.0))
                ranks[sj] = ranks[sj] + inc
        bias_ref[g] = jnp.where(jnp.concatenate(ranks, axis=0) < SLC_TOPN, 0.0, neg_inf)

    k_rel = lax.broadcasted_iota(jnp.int32, (ck, tq), 0)
    t_rel = lax.broadcasted_iota(jnp.int32, (ck, tq), 1)
    causal_bias = jnp.where(k_rel <= t_rel, 0.0, neg_inf)
    tail_bias = jnp.where(k_rel > t_rel, 0.0, neg_inf)

    def reset():
        m_ref[...] = jnp.full(m_ref.shape, neg_inf, F32)
        acc_ref[...] = jnp.zeros(acc_ref.shape, F32)

    def chunk(ci, k_ref, v_ref, extra_bias, selected):
        k = k_ref[0, pl.ds(pl.multiple_of(ci * ck, ck), ck), :]
        for g in range(N_KV):
            v_aug = jnp.concatenate([v_ref[0, ci, g * HEAD_DIM:(g + 1) * HEAD_DIM, :], ones_rows], axis=0)
            bias = extra_bias
            if selected:
                sel_bias = jnp.concatenate(
                    [jnp.broadcast_to(bias_ref[g, pl.ds(ci * bpc + jj, 1), :], (SLC_BLK, tq)) for jj in range(bpc)],
                    axis=0)
                bias = sel_bias if bias is None else sel_bias + bias
            s = jnp.dot(k, qg_ref[g], preferred_element_type=F32)
            if bias is not None:
                s = s + tile_h(bias)
            m_prev = m_ref[g]
            m_new = jnp.maximum(m_prev, jnp.max(s, axis=0, keepdims=True))
            m_safe = jnp.where(m_new == neg_inf, 0.0, m_new)
            alpha = jnp.exp(m_prev - m_safe)
            p = jnp.exp(s - m_safe).astype(BF16)
            acc_ref[g] = alpha * acc_ref[g] + jnp.dot(v_aug, p, preferred_element_type=F32)
            m_ref[g] = m_new

    def result():
        outs = []
        for g in range(N_KV):
            o = acc_ref[g, 0:HEAD_DIM] / jnp.maximum(acc_ref[g, HEAD_DIM:HEAD_DIM + 1], 1e-30)
            outs += [head(o, h) for h in range(HPG)]
        return outs

    reset()

    def sel_body(ci, carry):
        chunk(ci, ks_ref, vs_ref, None, True)
        return carry

    lax.fori_loop(0, qi, sel_body, 0)
    chunk(qi, ks_ref, vs_ref, causal_bias, True)
    o_slc = result()

    reset()
    n_back = WIN // ck

    @pl.when(qi >= n_back)
    def _():
        chunk(qi - n_back, kw_ref, vw_ref, tail_bias, False)

    for back in range(n_back - 1, 0, -1):
        @pl.when(qi >= back)
        def _(back=back):
            chunk(qi - back, kw_ref, vw_ref, None, False)

    chunk(qi, kw_ref, vw_ref, causal_bias, False)
    o_win = result()

    gates = gate_ref[0]
    heads = []
    for idx in range(N_HEADS):
        c = idx * 3
        heads.append(gates[c:c + 1] * o_cmp[idx] + gates[c + 1:c + 2] * o_slc[idx] + gates[c + 2:c + 3] * o_win[idx])
    y = jnp.concatenate(heads, axis=0)
    y = y * lax.rsqrt(jnp.mean(y * y, axis=0, keepdims=True) + EPS) * ow_ref[...]
    y_ref[0] = y.T.astype(BF16)


def _attention(q, kc, vc, ks, vs, kw, vw, gates, out_norm_w):
    B, _, T = q.shape
    ck = vs.shape[3]
    tq = ck
    n_rows = kc.shape[1]
    n_cmp = (T - CMP_LEN) // CMP_STRIDE + 1
    nb = T // SLC_BLK
    assert nb <= SLC_BLK and WIN % ck == 0 and T % tq == 0 and ck % SLC_BLK == 0
    cs = np.arange(n_cmp)[None, :] * CMP_STRIDE
    bs = np.arange(nb)[:, None] * SLC_BLK
    cover_t = np.clip(np.minimum(cs + CMP_LEN, bs + SLC_BLK) - np.maximum(cs, bs), 0, None) / CMP_LEN
    cover_t = jnp.asarray(np.pad(cover_t, ((0, SLC_BLK - nb), (0, n_rows - n_cmp))), BF16)

    per_b = lambda *s: pl.BlockSpec((1,) + s, lambda b, i: (b,) + (0,) * len(s))
    return pl.pallas_call(
        functools.partial(_attn_kernel, n_cmp=n_cmp),
        grid=(B, T // tq),
        in_specs=[pl.BlockSpec((1, D_ATTN, tq), lambda b, i: (b, 0, i)),
                  per_b(n_rows, KV_W), per_b(KV_W, n_rows),
                  per_b(T, KV_W), per_b(T // ck, KV_W, ck), per_b(T, KV_W), per_b(T // ck, KV_W, ck),
                  pl.BlockSpec((1, GATE_ROWS, tq), lambda b, i: (b, 0, i)),
                  _full((SLC_BLK, n_rows)), _full((D_ATTN, 1))],
        out_specs=pl.BlockSpec((1, tq, D_ATTN), lambda b, i: (b, i, 0)),
        out_shape=jax.ShapeDtypeStruct((B, T, D_ATTN), BF16),
        scratch_shapes=[pltpu.VMEM((N_KV, KV_W, HPG * tq), BF16),
                        pltpu.VMEM((N_KV, HEAD_DIM + 16, HPG * tq), F32), pltpu.VMEM((N_KV, 1, HPG * tq), F32),
                        pltpu.VMEM((N_KV, SLC_BLK, tq), F32)],
        compiler_params=_params(2),
        name="nsa_attention",
    )(q, kc, vc, ks, vs, kw, vw, gates, cover_t, out_norm_w.reshape(D_ATTN, 1))


def _out_ffn_kernel(x_ref, yr_ref, ya_ref, wo_r_ref, wo_a_ref, fw_ref, wg_ref, wu_ref, wd_ref, o_ref):
    x1 = (x_ref[...] + jnp.dot(yr_ref[...], wo_r_ref[...], preferred_element_type=F32)
          + jnp.dot(ya_ref[...], wo_a_ref[...], preferred_element_type=F32))
    h = (x1 * lax.rsqrt(jnp.mean(x1 * x1, axis=-1, keepdims=True) + EPS) * fw_ref[...]).astype(BF16)
    gate = jnp.dot(h, wg_ref[...], preferred_element_type=F32)
    up = jnp.dot(h, wu_ref[...], preferred_element_type=F32)
    act = (gate * jax.nn.sigmoid(gate) * up).astype(BF16)
    o_ref[...] = x1 + jnp.dot(act, wd_ref[...], preferred_element_type=F32)


def _out_ffn(x, y_rnn, y_att, w_out, ffn_norm_w, w_gate, w_up, w_down):
    B, T, D = x.shape
    n = B * T
    tm = min(FFN_TILE, n)
    d_ff = w_gate.shape[1]
    tok = lambda w: pl.BlockSpec((tm, w), lambda i: (i, 0))
    once = lambda shape: pl.BlockSpec(shape, lambda i: (0, 0), pipeline_mode=pl.Buffered(1))
    out = pl.pallas_call(
        _out_ffn_kernel,
        grid=(n // tm,),
        in_specs=[tok(D), tok(D_RNN), tok(D_ATTN), once((D_RNN, D)), once((D_ATTN, D)), once((1, D)),
                  once((D, d_ff)), once((D, d_ff)), once((d_ff, D))],
        out_specs=tok(D),
        out_shape=jax.ShapeDtypeStruct((n, D), F32),
        compiler_params=_params(1),
        name="out_ffn",
    )(x.reshape(n, D), y_rnn.reshape(n, D_RNN), y_att.reshape(n, D_ATTN),
      w_out[:D_RNN].astype(BF16), w_out[D_RNN:].astype(BF16), ffn_norm_w.reshape(1, D),
      w_gate.astype(BF16), w_up.astype(BF16), w_down.astype(BF16))
    return out.reshape(B, T, D)


def kernel(x, positions, attn_norm_w, w_in, conv_w, conv_b, gate_a_w, gate_a_b, gate_x_w, gate_x_b, lru_lambda, q_norm_w, k_norm_w, cmp_pos, cmp_k_w1, cmp_k_w2, cmp_v_w1, cmp_v_w2, rnn_out_norm_w, attn_out_norm_w, w_out, ffn_norm_w, w_gate, w_up, w_down):
    cos_t, sin_t = _rope_tables(positions)
    for l in range(attn_norm_w.shape[0]):
        xr, gr, kcvc, q, ks, kw, vs, vw, gates = _in_proj(x, attn_norm_w[l], w_in[l], q_norm_w[l], k_norm_w[l],
                                                          cos_t, sin_t)
        y_rnn = _rglru(xr, gr, conv_w[l], conv_b[l], gate_a_w[l], gate_a_b[l], gate_x_w[l], gate_x_b[l],
                       lru_lambda[l], rnn_out_norm_w[l])
        kc, vc = _compress(kcvc, cmp_pos[l], cmp_k_w1[l], cmp_k_w2[l], cmp_v_w1[l], cmp_v_w2[l], k_norm_w[l],
                           cos_t, sin_t)
        y_att = _attention(q, kc, vc, ks, vs, kw, vw, gates, attn_out_norm_w[l])
        x = _out_ffn(x, y_rnn, y_att, w_out[l], ffn_norm_w[l], w_gate[l], w_up[l], w_down[l])
    return x
```

```python
import functools

import numpy as np
import jax
import jax.numpy as jnp
from jax import lax
from jax.experimental import pallas as pl
from jax.experimental.pallas import tpu as pltpu

F32 = jnp.float32
BF16 = jnp.bfloat16

D_RNN = 512
RNN_BLOCKS = 8
CONV_WIDTH = 4
LRU_C = 8.0
N_HEADS = 8
N_KV = 2
HPG = N_HEADS // N_KV
HEAD_DIM = 64
D_ATTN = N_HEADS * HEAD_DIM
KV_W = N_KV * HEAD_DIM
CMP_LEN = 32
CMP_STRIDE = 16
SLC_BLK = 64
SLC_TOPN = 16
WIN = 512
FORCE_SCORE = 1.0e4
ROPE_THETA = 500000.0
ROPE_DIM = HEAD_DIM // 4
ROPE_HALF = ROPE_DIM // 2
EPS = 1e-6
ATTN_SCALE = HEAD_DIM ** -0.5
LOG2E = 1.4426950408889634
Q_SCALE = ATTN_SCALE * LOG2E
N_GATES = 3 * N_HEADS
GATE_ROWS = 32

LANES = 128
VMEM_LIMIT = 56 * 1024 * 1024

IN_TILE = 512
RNN_TILE = 256
ATT_TILE = 256
FFN_TILE = 512

_DN_T = (((1,), (1,)), ((), ()))


def _params(n_axes):
    return pltpu.CompilerParams(dimension_semantics=("arbitrary",) * n_axes,
                                vmem_limit_bytes=VMEM_LIMIT)


def _full(shape):
    return pl.BlockSpec(shape, lambda *_: (0,) * len(shape))


def _rope_table_kernel(pos_ref, inv_ref, cos_ref, sin_ref):
    ang = pos_ref[0].astype(F32) * inv_ref[...]
    cos_ref[0] = jnp.cos(ang)
    sin_ref[0] = jnp.sin(ang)


def _rope_tables(positions):
    B, T = positions.shape
    inv = jnp.power(ROPE_THETA, -jnp.arange(ROPE_HALF, dtype=F32) * 2.0 / ROPE_DIM).reshape(ROPE_HALF, 1)
    out = jax.ShapeDtypeStruct((B, ROPE_HALF, T), F32)
    return pl.pallas_call(
        _rope_table_kernel,
        grid=(B,),
        in_specs=[pl.BlockSpec((1, 1, T), lambda b: (b, 0, 0)), _full((ROPE_HALF, 1))],
        out_specs=[pl.BlockSpec((1, ROPE_HALF, T), lambda b: (b, 0, 0))] * 2,
        out_shape=[out, out],
        compiler_params=_params(1),
        name="rope_tables",
    )(positions.reshape(B, 1, T), inv)


def _norm_rope_t(blk, w_col, cos, sin):
    y = blk * lax.rsqrt(jnp.mean(blk * blk, axis=0, keepdims=True) + EPS) * w_col
    x1 = y[0:ROPE_HALF]
    x2 = y[ROPE_HALF:ROPE_DIM]
    return jnp.concatenate([x1 * cos - x2 * sin, x2 * cos + x1 * sin, y[ROPE_DIM:]], axis=0)


def _in_proj_kernel(x_ref, nw_ref, wn_ref, wt_ref, qw_ref, kw_ref, cos_ref, sin_ref,
                    xr_ref, gr_ref, kcvc_ref, q_ref, k_ref, v_ref, gate_ref):
    ck = v_ref.shape[4]
    tm = x_ref.shape[1]
    x = x_ref[0]
    h = x * lax.rsqrt(jnp.mean(x * x, axis=-1, keepdims=True) + EPS) * nw_ref[...]
    hb = h.astype(BF16)
    pn = jnp.dot(hb, wn_ref[...], preferred_element_type=F32)
    pt = lax.dot_general(wt_ref[...], hb, _DN_T, preferred_element_type=F32)
    xr_ref[0] = pn[:, 0:D_RNN]
    gr_ref[0] = pn[:, D_RNN:2 * D_RNN]
    kcvc_ref[0] = pn[:, 2 * D_RNN:2 * D_RNN + 2 * KV_W]
    cos = cos_ref[0]
    sin = sin_ref[0]
    qw = qw_ref[...]
    kw = kw_ref[...]
    for hd in range(N_HEADS):
        rows = slice(hd * HEAD_DIM, (hd + 1) * HEAD_DIM)
        q_ref[0, rows, :] = (_norm_rope_t(pt[rows], qw, cos, sin) * Q_SCALE).astype(BF16)
    for br in range(2):
        base = D_ATTN + br * KV_W
        k_t = jnp.concatenate([_norm_rope_t(pt[base + g * HEAD_DIM:base + (g + 1) * HEAD_DIM], kw, cos, sin)
                               for g in range(N_KV)], axis=0)
        k_ref[0, br] = k_t.T.astype(BF16)
        base = D_ATTN + (2 + br) * KV_W
        for c in range(tm // ck):
            v_ref[0, br, c] = pt[base:base + KV_W, c * ck:(c + 1) * ck].astype(BF16)
    gbase = D_ATTN + 4 * KV_W
    gate_ref[0] = jax.nn.sigmoid(pt[gbase:gbase + GATE_ROWS])


def _in_proj(x, attn_norm_w, w_in, q_norm_w, k_norm_w, cos_t, sin_t):
    B, T, D = x.shape
    tm = min(IN_TILE, T)
    ck = min(ATT_TILE, T)
    sp = np.cumsum((D_RNN, D_RNN, D_ATTN, KV_W, KV_W, KV_W, KV_W, KV_W, KV_W, N_GATES))
    xr_w, gr_w, q_w, kc_w, vc_w, ks_w, vs_w, kw_w, vw_w, gl_w = jnp.split(w_in, sp[:-1], axis=1)
    wn = jnp.concatenate([xr_w, gr_w, kc_w, vc_w], axis=1).astype(BF16)
    wt = jnp.concatenate([q_w, ks_w, kw_w, vs_w, vw_w, gl_w,
                          jnp.zeros((D, GATE_ROWS - N_GATES), w_in.dtype)], axis=1).T.astype(BF16)
    n_wn = wn.shape[1]
    n_wt = wt.shape[0]
    f32o = lambda *s: jax.ShapeDtypeStruct(s, F32)
    bfo = lambda *s: jax.ShapeDtypeStruct(s, BF16)
    tok = lambda w: pl.BlockSpec((1, tm, w), lambda b, i: (b, i, 0))
    feat = lambda r: pl.BlockSpec((1, r, tm), lambda b, i: (b, 0, i))
    keys = pl.BlockSpec((1, 2, tm, KV_W), lambda b, i: (b, 0, i, 0))
    vals = pl.BlockSpec((1, 2, tm // ck, KV_W, ck), lambda b, i: (b, 0, i, 0, 0))
    return pl.pallas_call(
        _in_proj_kernel,
        grid=(B, T // tm),
        in_specs=[tok(D), _full((1, D)), _full((D, n_wn)), _full((n_wt, D)),
                  _full((HEAD_DIM, 1)), _full((HEAD_DIM, 1)), feat(ROPE_HALF), feat(ROPE_HALF)],
        out_specs=[tok(D_RNN), tok(D_RNN), tok(2 * KV_W), feat(D_ATTN), keys, vals, feat(GATE_ROWS)],
        out_shape=[f32o(B, T, D_RNN), f32o(B, T, D_RNN), f32o(B, T, 2 * KV_W),
                   bfo(B, D_ATTN, T), bfo(B, 2, T, KV_W), bfo(B, 2, T // ck, KV_W, ck),
                   f32o(B, GATE_ROWS, T)],
        compiler_params=_params(2),
        name="in_proj",
    )(x, attn_norm_w.reshape(1, D), wn, wt, q_norm_w.reshape(HEAD_DIM, 1), k_norm_w.reshape(HEAD_DIM, 1),
      cos_t, sin_t)


def _rglru_kernel(xr_ref, gr_ref, cw_ref, cb_ref, wa_ref, ba_ref, wx_ref, bx_ref, lam_ref, ow_ref,
                  y_ref, ext_ref, h_ref):
    tt = xr_ref.shape[1]
    pad = 8

    @pl.when(pl.program_id(1) == 0)
    def _():
        ext_ref[0:pad, :] = jnp.zeros((pad, D_RNN), F32)
        h_ref[...] = jnp.zeros((1, D_RNN), F32)

    xr = xr_ref[0]
    ext_ref[pad:pad + tt, :] = xr
    xc = cb_ref[...]
    for k in range(CONV_WIDTH):
        off = pad - (CONV_WIDTH - 1) + k
        xc = xc + cw_ref[k:k + 1, :] * ext_ref[off:off + tt, :]
    ext_ref[0:pad, :] = xr[tt - pad:tt]
    xcb = xc.astype(BF16)
    r = jax.nn.sigmoid(jnp.dot(xcb, wa_ref[...], preferred_element_type=F32) + ba_ref[...])
    i = jax.nn.sigmoid(jnp.dot(xcb, wx_ref[...], preferred_element_type=F32) + bx_ref[...])
    z = -lam_ref[...]
    softplus = jnp.maximum(z, 0.0) + jnp.log1p(jnp.exp(-jnp.abs(z)))
    log_a = -LRU_C * r * softplus
    a = jnp.exp(log_a)
    t2 = 2.0 * log_a
    u = jnp.exp(t2)
    neg_expm1 = jnp.where(u == 1.0, -t2, (1.0 - u) * t2 / jnp.log(u))
    b = jnp.sqrt(neg_expm1) * (i * xc)
    row = lax.broadcasted_iota(jnp.int32, (tt, D_RNN), 0) & 7
    for s in (1, 2, 4):
        keep = row >= s
        a_sh = jnp.where(keep, pltpu.roll(a, s, 0), 1.0)
        b_sh = jnp.where(keep, pltpu.roll(b, s, 0), 0.0)
        b = a * b_sh + b
        a = a * a_sh
    h = h_ref[...]
    hs = []
    for g in range(tt // 8):
        hg = a[g * 8:(g + 1) * 8] * h + b[g * 8:(g + 1) * 8]
        h = hg[7:8]
        hs.append(hg)
    h_ref[...] = h
    y = jax.nn.gelu(gr_ref[0]) * jnp.concatenate(hs, axis=0)
    y = y * lax.rsqrt(jnp.mean(y * y, axis=-1, keepdims=True) + EPS) * ow_ref[...]
    y_ref[0] = y.astype(BF16)


def _block_diag(w):
    n, d, _ = w.shape
    eye = jnp.eye(n, dtype=w.dtype)
    return (eye[:, None, :, None] * w[:, :, None, :]).reshape(n * d, n * d)


def _rglru(xr, gr, conv_w, conv_b, gate_a_w, gate_a_b, gate_x_w, gate_x_b, lru_lambda, out_norm_w):
    B, T, _ = xr.shape
    tt = min(RNN_TILE, T)
    tok = pl.BlockSpec((1, tt, D_RNN), lambda b, i: (b, i, 0))
    row = lambda a: a.reshape(1, D_RNN)
    return pl.pallas_call(
        _rglru_kernel,
        grid=(B, T // tt),
        in_specs=[tok, tok, _full((CONV_WIDTH, D_RNN)), _full((1, D_RNN)),
                  _full((D_RNN, D_RNN)), _full((1, D_RNN)), _full((D_RNN, D_RNN)), _full((1, D_RNN)),
                  _full((1, D_RNN)), _full((1, D_RNN))],
        out_specs=tok,
        out_shape=jax.ShapeDtypeStruct((B, T, D_RNN), BF16),
        scratch_shapes=[pltpu.VMEM((tt + 8, D_RNN), F32), pltpu.VMEM((1, D_RNN), F32)],
        compiler_params=_params(2),
        name="rglru",
    )(xr, gr, conv_w, row(conv_b), _block_diag(gate_a_w).astype(BF16), row(gate_a_b),
      _block_diag(gate_x_w).astype(BF16), row(gate_x_b), row(lru_lambda), row(out_norm_w))


def _compress_kernel(rk_ref, rv_ref, pos_ref, w1k_ref, w2k_ref, w1v_ref, w2v_ref, kw_ref, cos_ref, sin_ref,
                     kc_ref, vc_ref):
    half = pos_ref.shape[1]
    n_rows = rk_ref.shape[2]

    def hidden(r, w1_ref):
        top = jnp.dot((r + pos_ref[0:1, :]).astype(BF16), w1_ref[0:half, :], preferred_element_type=F32)
        bot = jnp.dot((r + pos_ref[1:2, :]).astype(BF16), w1_ref[half:2 * half, :], preferred_element_type=F32)
        return jax.nn.gelu(top + pltpu.roll(bot, n_rows - 1, 0)).astype(BF16)

    kc_t = []
    vc_t = []
    for g in range(N_KV):
        k_t = lax.dot_general(w2k_ref[...], hidden(rk_ref[0, g], w1k_ref), _DN_T, preferred_element_type=F32)
        kc_t.append(_norm_rope_t(k_t, kw_ref[...], cos_ref[0], sin_ref[0]))
        vc_t.append(lax.dot_general(w2v_ref[...], hidden(rv_ref[0, g], w1v_ref), _DN_T,
                                    preferred_element_type=F32))
    kc_ref[0] = jnp.concatenate(kc_t, axis=0).T.astype(BF16)
    vc_ref[0] = jnp.concatenate(vc_t, axis=0).astype(BF16)


def _compress(kcvc, cmp_pos, k_w1, k_w2, v_w1, v_w2, k_norm_w, cos_t, sin_t):
    B, T, _ = kcvc.shape
    n_rows = T // CMP_STRIDE
    n_cmp = (T - CMP_LEN) // CMP_STRIDE + 1
    half = CMP_STRIDE * HEAD_DIM
    hid = k_w1.shape[1]

    def rows(tokens):
        return tokens.reshape(B, T, N_KV, HEAD_DIM).transpose(0, 2, 1, 3).reshape(B, N_KV, n_rows, half)

    def at_block_end(tab):
        tab = tab[:, :, CMP_LEN - 1::CMP_STRIDE]
        return jnp.pad(tab, ((0, 0), (0, 0), (0, n_rows - n_cmp)))

    blk = pl.BlockSpec((1, N_KV, n_rows, half), lambda b: (b, 0, 0, 0))
    rope = pl.BlockSpec((1, ROPE_HALF, n_rows), lambda b: (b, 0, 0))
    return pl.pallas_call(
        _compress_kernel,
        grid=(B,),
        in_specs=[blk, blk, _full((2, half)), _full((2 * half, hid)), _full((HEAD_DIM, hid)),
                  _full((2 * half, hid)), _full((HEAD_DIM, hid)), _full((HEAD_DIM, 1)), rope, rope],
        out_specs=[pl.BlockSpec((1, n_rows, KV_W), lambda b: (b, 0, 0)),
                   pl.BlockSpec((1, KV_W, n_rows), lambda b: (b, 0, 0))],
        out_shape=[jax.ShapeDtypeStruct((B, n_rows, KV_W), BF16), jax.ShapeDtypeStruct((B, KV_W, n_rows), BF16)],
        compiler_params=_params(1),
        name="compress",
    )(rows(kcvc[:, :, :KV_W]), rows(kcvc[:, :, KV_W:]), cmp_pos.reshape(2, half),
      k_w1.astype(BF16), k_w2.T.astype(BF16), v_w1.astype(BF16), v_w2.T.astype(BF16),
      k_norm_w.reshape(HEAD_DIM, 1), at_block_end(cos_t), at_block_end(sin_t))


def _attn_kernel(q_ref, kc_ref, vc_ref, k_ref, v_ref, gate_ref, cover_ref, ow_ref,
                 y_ref, qg_ref, s_ref, acc_ref, m_ref, bias_ref, *, n_cmp):
    tq = q_ref.shape[2]
    ck = v_ref.shape[4]
    n_rows = kc_ref.shape[1]
    n_blk = SLC_BLK
    bpc = ck // SLC_BLK
    l_rows = 16
    qi = pl.program_id(1)
    q0 = qi * tq
    neg_inf = -jnp.inf
    t_row = q0 + lax.broadcasted_iota(jnp.int32, (1, tq), 1)
    ones_rows = jnp.ones((l_rows, ck), BF16)
    tile_h = lambda a: jnp.concatenate([a] * HPG, axis=1)
    head = lambda a, h: a[:, h * tq:(h + 1) * tq]

    zero_half = jnp.zeros((HEAD_DIM, HPG * tq), BF16)
    for g in range(N_KV):
        qg = jnp.concatenate([q_ref[0, (g * HPG + h) * HEAD_DIM:(g * HPG + h + 1) * HEAD_DIM, :]
                              for h in range(HPG)], axis=1)
        qg_ref[g] = jnp.concatenate([qg, zero_half] if g == 0 else [zero_half, qg], axis=0)

    c_col = lax.broadcasted_iota(jnp.int32, (n_rows, tq), 0)
    cmask = tile_h(((c_col * CMP_STRIDE + (CMP_LEN - 1)) <= t_row) & (c_col < n_cmp))
    j_col = lax.broadcasted_iota(jnp.int32, (n_blk, tq), 0)
    cur = t_row // SLC_BLK
    forced = (j_col == 0) | (j_col == cur) | (j_col == cur - 1)
    valid = (j_col * SLC_BLK) <= t_row
    sub_row = lax.broadcasted_iota(jnp.int32, (8, tq), 0)
    kc = kc_ref[0]
    o_cmp = []
    for g in range(N_KV):
        v_cov = jnp.concatenate([vc_ref[0, g * HEAD_DIM:(g + 1) * HEAD_DIM, :], cover_ref[...]], axis=0)
        s = jnp.dot(kc, qg_ref[g], preferred_element_type=F32)
        s = jnp.where(cmask, s, neg_inf)
        m = jnp.max(s, axis=0, keepdims=True)
        m = jnp.where(m == neg_inf, 0.0, m)
        e = jnp.where(cmask, jnp.exp2(s - m), 0.0)
        pb = (e / jnp.maximum(jnp.sum(e, axis=0, keepdims=True), 1e-30)).astype(BF16)
        r = jnp.dot(v_cov, pb, preferred_element_type=F32)
        o_cmp += [head(r[0:HEAD_DIM], h) for h in range(HPG)]
        imp = head(r[HEAD_DIM:], 0)
        for h in range(1, HPG):
            imp = imp + head(r[HEAD_DIM:], h)
        score = jnp.where(forced, FORCE_SCORE, jnp.where(valid, imp, neg_inf))
        slabs = [score[8 * sj:8 * sj + 8] for sj in range(n_blk // 8)]
        ranks = [jnp.zeros((8, tq), F32) for _ in slabs]
        for jp in range(n_blk):
            other = jnp.broadcast_to(score[jp:jp + 1, :], (8, tq))
            for sj, slab in enumerate(slabs):
                if jp < 8 * sj:
                    inc = jnp.where(other >= slab, 1.0, 0.0)
                elif jp >= 8 * sj + 8:
                    inc = jnp.where(other > slab, 1.0, 0.0)
                else:
                    tie = jnp.where(sub_row > jp - 8 * sj, 1.0, 0.0)
                    inc = jnp.where(other > slab, 1.0, jnp.where(other == slab, tie, 0.0))
                ranks[sj] = ranks[sj] + inc
        bias_ref[g, 0:n_blk, :] = jnp.where(jnp.concatenate(ranks, axis=0) < SLC_TOPN, 0.0, neg_inf)
        bias_ref[g, n_blk:n_blk + 8, :] = jnp.zeros((8, tq), F32)

    n_back = WIN // ck
    n_sel = qi + 1
    n_win = jnp.minimum(qi, n_back) + 1
    n_items = n_sel + n_win
    k_rel = lax.broadcasted_iota(jnp.int32, (ck, tq), 0)
    t_rel = lax.broadcasted_iota(jnp.int32, (ck, tq), 1)
    causal_bias = jnp.where(k_rel <= t_rel, 0.0, neg_inf)
    tail_bias = jnp.where(k_rel > t_rel, 0.0, neg_inf)
    m_ref[...] = jnp.full(m_ref.shape, neg_inf, F32)
    acc_ref[...] = jnp.zeros(acc_ref.shape, F32)

    def item(n):
        br = (n >= n_sel).astype(jnp.int32)
        return br, jnp.minimum(jnp.where(br == 1, qi - n_win + 1 + (n - n_sel), n), qi)

    def key_chunk(n):
        br, ci = item(n)
        return k_ref[0, br, pl.ds(pl.multiple_of(ci * ck, ck), ck), :]

    def scores(k, slot, g, h):
        cols = slice(h * tq, (h + 1) * tq)
        s_ref[slot, g, :, cols] = jnp.dot(k, qg_ref[g, :, cols], preferred_element_type=F32)

    def step(n, slot, nxt_slot):
        br, ci = item(n)
        k_next = key_chunk(n + 1)
        rel = ci - qi
        extra = jnp.where(rel == 0, causal_bias, jnp.where((br == 1) & (rel == -n_back), tail_bias, 0.0))
        extra = jnp.where(n >= n_items, neg_inf, extra)
        for g in range(N_KV):
            v_aug = jnp.concatenate([v_ref[0, br, ci, g * HEAD_DIM:(g + 1) * HEAD_DIM, :], ones_rows], axis=0)
            bias = extra + jnp.concatenate(
                [jnp.broadcast_to(bias_ref[g, pl.ds(jnp.where(br == 0, ci * bpc + jj, n_blk), 1), :], (SLC_BLK, tq))
                 for jj in range(bpc)], axis=0)
            for h in range(HPG):
                cols = slice(h * tq, (h + 1) * tq)
                scores(k_next, nxt_slot, g, h)
                s = s_ref[slot, g, :, cols] + bias
                m_prev = m_ref[br, g, :, cols]
                m_new = jnp.maximum(m_prev, jnp.max(s, axis=0, keepdims=True))
                m_safe = jnp.where(m_new == neg_inf, 0.0, m_new)
                alpha = jnp.exp2(m_prev - m_safe)
                p = jnp.exp2(s - m_safe).astype(BF16)
                acc_ref[br, g, :, cols] = (alpha * acc_ref[br, g, :, cols]
                                           + jnp.dot(v_aug, p, preferred_element_type=F32))
                m_ref[br, g, :, cols] = m_new

    k_first = key_chunk(0)
    for g in range(N_KV):
        for h in range(HPG):
            scores(k_first, 0, g, h)

    def body(pair, carry):
        step(2 * pair, 0, 1)
        step(2 * pair + 1, 1, 0)
        return carry

    lax.fori_loop(0, (n_items + 1) // 2, body, 0)

    def result(br):
        outs = []
        for g in range(N_KV):
            o = acc_ref[br, g, 0:HEAD_DIM] / jnp.maximum(acc_ref[br, g, HEAD_DIM:HEAD_DIM + 1], 1e-30)
            outs += [head(o, h) for h in range(HPG)]
        return outs

    o_slc = result(0)
    o_win = result(1)

    gates = gate_ref[0]
    heads = []
    for idx in range(N_HEADS):
        c = idx * 3
        heads.append(gates[c:c + 1] * o_cmp[idx] + gates[c + 1:c + 2] * o_slc[idx] + gates[c + 2:c + 3] * o_win[idx])
    y = jnp.concatenate(heads, axis=0)
    y = y * lax.rsqrt(jnp.mean(y * y, axis=0, keepdims=True) + EPS) * ow_ref[...]
    y_ref[0] = y.T.astype(BF16)


def _attention(q, kc, vc, k, v, gates, out_norm_w):
    B, _, T = q.shape
    ck = v.shape[4]
    tq = ck
    n_rows = kc.shape[1]
    n_cmp = (T - CMP_LEN) // CMP_STRIDE + 1
    nb = T // SLC_BLK
    assert nb <= SLC_BLK and WIN % ck == 0 and T % tq == 0 and ck % SLC_BLK == 0
    cs = np.arange(n_cmp)[None, :] * CMP_STRIDE
    bs = np.arange(nb)[:, None] * SLC_BLK
    cover_t = np.clip(np.minimum(cs + CMP_LEN, bs + SLC_BLK) - np.maximum(cs, bs), 0, None) / CMP_LEN
    cover_t = jnp.asarray(np.pad(cover_t, ((0, SLC_BLK - nb), (0, n_rows - n_cmp))), BF16)

    per_b = lambda *s: pl.BlockSpec((1,) + s, lambda b, i: (b,) + (0,) * len(s))
    return pl.pallas_call(
        functools.partial(_attn_kernel, n_cmp=n_cmp),
        grid=(B, T // tq),
        in_specs=[pl.BlockSpec((1, D_ATTN, tq), lambda b, i: (b, 0, i)),
                  per_b(n_rows, KV_W), per_b(KV_W, n_rows),
                  per_b(2, T, KV_W), per_b(2, T // ck, KV_W, ck),
                  pl.BlockSpec((1, GATE_ROWS, tq), lambda b, i: (b, 0, i)),
                  _full((SLC_BLK, n_rows)), _full((D_ATTN, 1))],
        out_specs=pl.BlockSpec((1, tq, D_ATTN), lambda b, i: (b, i, 0)),
        out_shape=jax.ShapeDtypeStruct((B, T, D_ATTN), BF16),
        scratch_shapes=[pltpu.VMEM((N_KV, KV_W, HPG * tq), BF16),
                        pltpu.VMEM((2, N_KV, ck, HPG * tq), F32),
                        pltpu.VMEM((2, N_KV, HEAD_DIM + 16, HPG * tq), F32),
                        pltpu.VMEM((2, N_KV, 1, HPG * tq), F32),
                        pltpu.VMEM((N_KV, SLC_BLK + 8, tq), F32)],
        compiler_params=_params(2),
        name="nsa_attention",
    )(q, kc, vc, k, v, gates, cover_t, out_norm_w.reshape(D_ATTN, 1))


def _out_ffn_kernel(x_ref, yr_ref, ya_ref, wo_r_ref, wo_a_ref, fw_ref, wg_ref, wu_ref, wd_ref, o_ref):
    x1 = (x_ref[...] + jnp.dot(yr_ref[...], wo_r_ref[...], preferred_element_type=F32)
          + jnp.dot(ya_ref[...], wo_a_ref[...], preferred_element_type=F32))
    h = (x1 * lax.rsqrt(jnp.mean(x1 * x1, axis=-1, keepdims=True) + EPS) * fw_ref[...]).astype(BF16)
    gate = jnp.dot(h, wg_ref[...], preferred_element_type=F32)
    up = jnp.dot(h, wu_ref[...], preferred_element_type=F32)
    act = (gate * jax.nn.sigmoid(gate) * up).astype(BF16)
    o_ref[...] = x1 + jnp.dot(act, wd_ref[...], preferred_element_type=F32)


def _out_ffn(x, y_rnn, y_att, w_out, ffn_norm_w, w_gate, w_up, w_down):
    B, T, D = x.shape
    n = B * T
    tm = min(FFN_TILE, n)
    d_ff = w_gate.shape[1]
    tok = lambda w: pl.BlockSpec((tm, w), lambda i: (i, 0))
    once = lambda shape: pl.BlockSpec(shape, lambda i: (0, 0), pipeline_mode=pl.Buffered(1))
    out = pl.pallas_call(
        _out_ffn_kernel,
        grid=(n // tm,),
        in_specs=[tok(D), tok(D_RNN), tok(D_ATTN), once((D_RNN, D)), once((D_ATTN, D)), once((1, D)),
                  once((D, d_ff)), once((D, d_ff)), once((d_ff, D))],
        out_specs=tok(D),
        out_shape=jax.ShapeDtypeStruct((n, D), F32),
        compiler_params=_params(1),
        name="out_ffn",
    )(x.reshape(n, D), y_rnn.reshape(n, D_RNN), y_att.reshape(n, D_ATTN),
      w_out[:D_RNN].astype(BF16), w_out[D_RNN:].astype(BF16), ffn_norm_w.reshape(1, D),
      w_gate.astype(BF16), w_up.astype(BF16), w_down.astype(BF16))
    return out.reshape(B, T, D)


def kernel(x, positions, attn_norm_w, w_in, conv_w, conv_b, gate_a_w, gate_a_b, gate_x_w, gate_x_b, lru_lambda, q_norm_w, k_norm_w, cmp_pos, cmp_k_w1, cmp_k_w2, cmp_v_w1, cmp_v_w2, rnn_out_norm_w, attn_out_norm_w, w_out, ffn_norm_w, w_gate, w_up, w_down):
    cos_t, sin_t = _rope_tables(positions)
    for l in range(attn_norm_w.shape[0]):
        xr, gr, kcvc, q, k, v, gates = _in_proj(x, attn_norm_w[l], w_in[l], q_norm_w[l], k_norm_w[l], cos_t, sin_t)
        y_rnn = _rglru(xr, gr, conv_w[l], conv_b[l], gate_a_w[l], gate_a_b[l], gate_x_w[l], gate_x_b[l],
                       lru_lambda[l], rnn_out_norm_w[l])
        kc, vc = _compress(kcvc, cmp_pos[l], cmp_k_w1[l], cmp_k_w2[l], cmp_v_w1[l], cmp_v_w2[l], k_norm_w[l],
                           cos_t, sin_t)
        y_att = _attention(q, kc, vc, k, v, gates, attn_out_norm_w[l])
        x = _out_ffn(x, y_rnn, y_att, w_out[l], ffn_norm_w[l], w_gate[l], w_up[l], w_down[l])
    return x
```

```python
import functools

import numpy as np
import jax
import jax.numpy as jnp
from jax import lax
from jax.experimental import pallas as pl
from jax.experimental.pallas import tpu as pltpu

F32 = jnp.float32
BF16 = jnp.bfloat16

D_RNN = 512
RNN_BLOCKS = 8
CONV_WIDTH = 4
LRU_C = 8.0
N_HEADS = 8
N_KV = 2
HPG = N_HEADS // N_KV
HEAD_DIM = 64
D_ATTN = N_HEADS * HEAD_DIM
KV_W = N_KV * HEAD_DIM
CMP_LEN = 32
CMP_STRIDE = 16
SLC_BLK = 64
SLC_TOPN = 16
WIN = 512
FORCE_SCORE = 1.0e4
ROPE_THETA = 500000.0
ROPE_DIM = HEAD_DIM // 4
ROPE_HALF = ROPE_DIM // 2
EPS = 1e-6
ATTN_SCALE = HEAD_DIM ** -0.5
LOG2E = 1.4426950408889634
Q_SCALE = ATTN_SCALE * LOG2E
N_GATES = 3 * N_HEADS
GATE_ROWS = 32

MASK_BIG = 1e30
MAX_SCORE_BOUND = 40.0
F32_TINY = 1.1754944e-38

LANES = 128
VMEM_LIMIT = 56 * 1024 * 1024

IN_TILE = 512
RNN_TILE = 256
ATT_TILE = 256
FFN_TILE = 512

_DN_T = (((1,), (1,)), ((), ()))


def _params(n_axes):
    return pltpu.CompilerParams(dimension_semantics=("arbitrary",) * n_axes,
                                vmem_limit_bytes=VMEM_LIMIT)


def _full(shape):
    return pl.BlockSpec(shape, lambda *_: (0,) * len(shape))


def _rope_table_kernel(pos_ref, inv_ref, cos_ref, sin_ref):
    ang = pos_ref[0].astype(F32) * inv_ref[...]
    cos_ref[0] = jnp.cos(ang)
    sin_ref[0] = jnp.sin(ang)


def _rope_tables(positions):
    B, T = positions.shape
    inv = jnp.power(ROPE_THETA, -jnp.arange(ROPE_HALF, dtype=F32) * 2.0 / ROPE_DIM).reshape(ROPE_HALF, 1)
    out = jax.ShapeDtypeStruct((B, ROPE_HALF, T), F32)
    return pl.pallas_call(
        _rope_table_kernel,
        grid=(B,),
        in_specs=[pl.BlockSpec((1, 1, T), lambda b: (b, 0, 0)), _full((ROPE_HALF, 1))],
        out_specs=[pl.BlockSpec((1, ROPE_HALF, T), lambda b: (b, 0, 0))] * 2,
        out_shape=[out, out],
        compiler_params=_params(1),
        name="rope_tables",
    )(positions.reshape(B, 1, T), inv)


def _norm_rope_t(blk, w_col, cos, sin):
    y = blk * lax.rsqrt(jnp.mean(blk * blk, axis=0, keepdims=True) + EPS) * w_col
    x1 = y[0:ROPE_HALF]
    x2 = y[ROPE_HALF:ROPE_DIM]
    return jnp.concatenate([x1 * cos - x2 * sin, x2 * cos + x1 * sin, y[ROPE_DIM:]], axis=0)


def _in_proj_kernel(x_ref, nw_ref, wn_ref, wt_ref, qw_ref, kw_ref, cos_ref, sin_ref,
                    xr_ref, gr_ref, kcvc_ref, q_ref, k_ref, v_ref, gate_ref):
    ck = v_ref.shape[4]
    tm = x_ref.shape[1]
    x = x_ref[0]
    h = x * lax.rsqrt(jnp.mean(x * x, axis=-1, keepdims=True) + EPS) * nw_ref[...]
    hb = h.astype(BF16)
    pn = jnp.dot(hb, wn_ref[...], preferred_element_type=F32)
    pt = lax.dot_general(wt_ref[...], hb, _DN_T, preferred_element_type=F32)
    xr_ref[0] = pn[:, 0:D_RNN]
    gr_ref[0] = pn[:, D_RNN:2 * D_RNN]
    kcvc_ref[0] = pn[:, 2 * D_RNN:2 * D_RNN + 2 * KV_W]
    cos = cos_ref[0]
    sin = sin_ref[0]
    qw = qw_ref[...]
    kw = kw_ref[...]
    for hd in range(N_HEADS):
        rows = slice(hd * HEAD_DIM, (hd + 1) * HEAD_DIM)
        q_ref[0, rows, :] = (_norm_rope_t(pt[rows], qw, cos, sin) * Q_SCALE).astype(BF16)
    for br in range(2):
        base = D_ATTN + br * KV_W
        k_t = jnp.concatenate([_norm_rope_t(pt[base + g * HEAD_DIM:base + (g + 1) * HEAD_DIM], kw, cos, sin)
                               for g in range(N_KV)], axis=0)
        k_ref[0, br] = k_t.T.astype(BF16)
        base = D_ATTN + (2 + br) * KV_W
        for c in range(tm // ck):
            v_ref[0, br, c] = pt[base:base + KV_W, c * ck:(c + 1) * ck].astype(BF16)
    gbase = D_ATTN + 4 * KV_W
    gate_ref[0] = jax.nn.sigmoid(pt[gbase:gbase + GATE_ROWS])


def _in_proj(x, attn_norm_w, w_in, q_norm_w, k_norm_w, cos_t, sin_t):
    B, T, D = x.shape
    tm = min(IN_TILE, T)
    ck = min(ATT_TILE, T)
    sp = np.cumsum((D_RNN, D_RNN, D_ATTN, KV_W, KV_W, KV_W, KV_W, KV_W, KV_W, N_GATES))
    xr_w, gr_w, q_w, kc_w, vc_w, ks_w, vs_w, kw_w, vw_w, gl_w = jnp.split(w_in, sp[:-1], axis=1)
    wn = jnp.concatenate([xr_w, gr_w, kc_w, vc_w], axis=1).astype(BF16)
    wt = jnp.concatenate([q_w, ks_w, kw_w, vs_w, vw_w, gl_w,
                          jnp.zeros((D, GATE_ROWS - N_GATES), w_in.dtype)], axis=1).T.astype(BF16)
    n_wn = wn.shape[1]
    n_wt = wt.shape[0]
    f32o = lambda *s: jax.ShapeDtypeStruct(s, F32)
    bfo = lambda *s: jax.ShapeDtypeStruct(s, BF16)
    tok = lambda w: pl.BlockSpec((1, tm, w), lambda b, i: (b, i, 0))
    feat = lambda r: pl.BlockSpec((1, r, tm), lambda b, i: (b, 0, i))
    keys = pl.BlockSpec((1, 2, tm, KV_W), lambda b, i: (b, 0, i, 0))
    vals = pl.BlockSpec((1, 2, tm // ck, KV_W, ck), lambda b, i: (b, 0, i, 0, 0))
    return pl.pallas_call(
        _in_proj_kernel,
        grid=(B, T // tm),
        in_specs=[tok(D), _full((1, D)), _full((D, n_wn)), _full((n_wt, D)),
                  _full((HEAD_DIM, 1)), _full((HEAD_DIM, 1)), feat(ROPE_HALF), feat(ROPE_HALF)],
        out_specs=[tok(D_RNN), tok(D_RNN), tok(2 * KV_W), feat(D_ATTN), keys, vals, feat(GATE_ROWS)],
        out_shape=[f32o(B, T, D_RNN), f32o(B, T, D_RNN), f32o(B, T, 2 * KV_W),
                   bfo(B, D_ATTN, T), bfo(B, 2, T, KV_W), bfo(B, 2, T // ck, KV_W, ck),
                   f32o(B, GATE_ROWS, T)],
        compiler_params=_params(2),
        name="in_proj",
    )(x, attn_norm_w.reshape(1, D), wn, wt, q_norm_w.reshape(HEAD_DIM, 1), k_norm_w.reshape(HEAD_DIM, 1),
      cos_t, sin_t)


def _rglru_kernel(xr_ref, gr_ref, cw_ref, cb_ref, wa_ref, ba_ref, wx_ref, bx_ref, lam_ref, ow_ref,
                  y_ref, ext_ref, h_ref):
    tt = xr_ref.shape[1]
    pad = 8

    @pl.when(pl.program_id(1) == 0)
    def _():
        ext_ref[0:pad, :] = jnp.zeros((pad, D_RNN), F32)
        h_ref[...] = jnp.zeros((1, D_RNN), F32)

    xr = xr_ref[0]
    ext_ref[pad:pad + tt, :] = xr
    xc = cb_ref[...]
    for k in range(CONV_WIDTH):
        off = pad - (CONV_WIDTH - 1) + k
        xc = xc + cw_ref[k:k + 1, :] * ext_ref[off:off + tt, :]
    ext_ref[0:pad, :] = xr[tt - pad:tt]
    xcb = xc.astype(BF16)
    r = jax.nn.sigmoid(jnp.dot(xcb, wa_ref[...], preferred_element_type=F32) + ba_ref[...])
    i = jax.nn.sigmoid(jnp.dot(xcb, wx_ref[...], preferred_element_type=F32) + bx_ref[...])
    z = -lam_ref[...]
    softplus = jnp.maximum(z, 0.0) + jnp.log1p(jnp.exp(-jnp.abs(z)))
    log_a = -LRU_C * r * softplus
    a = jnp.exp(log_a)
    t2 = 2.0 * log_a
    u = jnp.exp(t2)
    neg_expm1 = jnp.where(u == 1.0, -t2, (1.0 - u) * t2 / jnp.log(u))
    b = jnp.sqrt(neg_expm1) * (i * xc)
    row = lax.broadcasted_iota(jnp.int32, (tt, D_RNN), 0) & 7
    for s in (1, 2, 4):
        keep = row >= s
        a_sh = jnp.where(keep, pltpu.roll(a, s, 0), 1.0)
        b_sh = jnp.where(keep, pltpu.roll(b, s, 0), 0.0)
        b = a * b_sh + b
        a = a * a_sh
    h = h_ref[...]
    hs = []
    for g in range(tt // 8):
        hg = a[g * 8:(g + 1) * 8] * h + b[g * 8:(g + 1) * 8]
        h = hg[7:8]
        hs.append(hg)
    h_ref[...] = h
    y = jax.nn.gelu(gr_ref[0]) * jnp.concatenate(hs, axis=0)
    y = y * lax.rsqrt(jnp.mean(y * y, axis=-1, keepdims=True) + EPS) * ow_ref[...]
    y_ref[0] = y.astype(BF16)


def _block_diag(w):
    n, d, _ = w.shape
    eye = jnp.eye(n, dtype=w.dtype)
    return (eye[:, None, :, None] * w[:, :, None, :]).reshape(n * d, n * d)


def _rglru(xr, gr, conv_w, conv_b, gate_a_w, gate_a_b, gate_x_w, gate_x_b, lru_lambda, out_norm_w):
    B, T, _ = xr.shape
    tt = min(RNN_TILE, T)
    tok = pl.BlockSpec((1, tt, D_RNN), lambda b, i: (b, i, 0))
    row = lambda a: a.reshape(1, D_RNN)
    return pl.pallas_call(
        _rglru_kernel,
        grid=(B, T // tt),
        in_specs=[tok, tok, _full((CONV_WIDTH, D_RNN)), _full((1, D_RNN)),
                  _full((D_RNN, D_RNN)), _full((1, D_RNN)), _full((D_RNN, D_RNN)), _full((1, D_RNN)),
                  _full((1, D_RNN)), _full((1, D_RNN))],
        out_specs=tok,
        out_shape=jax.ShapeDtypeStruct((B, T, D_RNN), BF16),
        scratch_shapes=[pltpu.VMEM((tt + 8, D_RNN), F32), pltpu.VMEM((1, D_RNN), F32)],
        compiler_params=_params(2),
        name="rglru",
    )(xr, gr, conv_w, row(conv_b), _block_diag(gate_a_w).astype(BF16), row(gate_a_b),
      _block_diag(gate_x_w).astype(BF16), row(gate_x_b), row(lru_lambda), row(out_norm_w))


def _compress_kernel(rk_ref, rv_ref, pos_ref, w1k_ref, w2k_ref, w1v_ref, w2v_ref, kw_ref, cos_ref, sin_ref,
                     kc_ref, vc_ref):
    half = pos_ref.shape[1]
    n_rows = rk_ref.shape[2]

    def hidden(r, w1_ref):
        top = jnp.dot((r + pos_ref[0:1, :]).astype(BF16), w1_ref[0:half, :], preferred_element_type=F32)
        bot = jnp.dot((r + pos_ref[1:2, :]).astype(BF16), w1_ref[half:2 * half, :], preferred_element_type=F32)
        return jax.nn.gelu(top + pltpu.roll(bot, n_rows - 1, 0)).astype(BF16)

    kc_t = []
    vc_t = []
    for g in range(N_KV):
        k_t = lax.dot_general(w2k_ref[...], hidden(rk_ref[0, g], w1k_ref), _DN_T, preferred_element_type=F32)
        kc_t.append(_norm_rope_t(k_t, kw_ref[...], cos_ref[0], sin_ref[0]))
        vc_t.append(lax.dot_general(w2v_ref[...], hidden(rv_ref[0, g], w1v_ref), _DN_T,
                                    preferred_element_type=F32))
    kc_ref[0] = jnp.concatenate(kc_t, axis=0).T.astype(BF16)
    vc_ref[0] = jnp.concatenate(vc_t, axis=0).astype(BF16)


def _compress(kcvc, cmp_pos, k_w1, k_w2, v_w1, v_w2, k_norm_w, cos_t, sin_t):
    B, T, _ = kcvc.shape
    n_rows = T // CMP_STRIDE
    n_cmp = (T - CMP_LEN) // CMP_STRIDE + 1
    half = CMP_STRIDE * HEAD_DIM
    hid = k_w1.shape[1]

    def rows(tokens):
        return tokens.reshape(B, T, N_KV, HEAD_DIM).transpose(0, 2, 1, 3).reshape(B, N_KV, n_rows, half)

    def at_block_end(tab):
        tab = tab[:, :, CMP_LEN - 1::CMP_STRIDE]
        return jnp.pad(tab, ((0, 0), (0, 0), (0, n_rows - n_cmp)))

    blk = pl.BlockSpec((1, N_KV, n_rows, half), lambda b: (b, 0, 0, 0))
    rope = pl.BlockSpec((1, ROPE_HALF, n_rows), lambda b: (b, 0, 0))
    return pl.pallas_call(
        _compress_kernel,
        grid=(B,),
        in_specs=[blk, blk, _full((2, half)), _full((2 * half, hid)), _full((HEAD_DIM, hid)),
                  _full((2 * half, hid)), _full((HEAD_DIM, hid)), _full((HEAD_DIM, 1)), rope, rope],
        out_specs=[pl.BlockSpec((1, n_rows, KV_W), lambda b: (b, 0, 0)),
                   pl.BlockSpec((1, KV_W, n_rows), lambda b: (b, 0, 0))],
        out_shape=[jax.ShapeDtypeStruct((B, n_rows, KV_W), BF16), jax.ShapeDtypeStruct((B, KV_W, n_rows), BF16)],
        compiler_params=_params(1),
        name="compress",
    )(rows(kcvc[:, :, :KV_W]), rows(kcvc[:, :, KV_W:]), cmp_pos.reshape(2, half),
      k_w1.astype(BF16), k_w2.T.astype(BF16), v_w1.astype(BF16), v_w2.T.astype(BF16),
      k_norm_w.reshape(HEAD_DIM, 1), at_block_end(cos_t), at_block_end(sin_t))


def _attn_kernel(bound_ref, q_ref, kc_ref, vc_ref, k_ref, v_ref, gate_ref, cover_ref, ow_ref,
                 y_ref, qg_ref, s_ref, acc_ref, m_ref, bias_ref, *, n_cmp, bounded):
    tq = q_ref.shape[2]
    ck = v_ref.shape[4]
    n_rows = kc_ref.shape[1]
    n_blk = SLC_BLK
    bpc = ck // SLC_BLK
    l_rows = 16
    qi = pl.program_id(1)
    q0 = qi * tq
    neg_inf = -jnp.inf
    masked = -MASK_BIG if bounded else neg_inf
    t_row = q0 + lax.broadcasted_iota(jnp.int32, (1, tq), 1)
    ones_rows = jnp.ones((l_rows, ck), BF16)
    tile_h = lambda a: jnp.concatenate([a] * HPG, axis=1)
    head = lambda a, h: a[:, h * tq:(h + 1) * tq]

    zero_half = jnp.zeros((HEAD_DIM, HPG * tq), BF16)
    for g in range(N_KV):
        qg = jnp.concatenate([q_ref[0, (g * HPG + h) * HEAD_DIM:(g * HPG + h + 1) * HEAD_DIM, :]
                              for h in range(HPG)], axis=1)
        qg_ref[g] = jnp.concatenate([qg, zero_half] if g == 0 else [zero_half, qg], axis=0)

    c_col = lax.broadcasted_iota(jnp.int32, (n_rows, tq), 0)
    cmask = tile_h(((c_col * CMP_STRIDE + (CMP_LEN - 1)) <= t_row) & (c_col < n_cmp))
    j_col = lax.broadcasted_iota(jnp.int32, (n_blk, tq), 0)
    cur = t_row // SLC_BLK
    forced = (j_col == 0) | (j_col == cur) | (j_col == cur - 1)
    valid = (j_col * SLC_BLK) <= t_row
    sub_row = lax.broadcasted_iota(jnp.int32, (8, tq), 0)
    kc = kc_ref[0]
    o_cmp = []
    for g in range(N_KV):
        v_cov = jnp.concatenate([vc_ref[0, g * HEAD_DIM:(g + 1) * HEAD_DIM, :], cover_ref[...]], axis=0)
        s = jnp.dot(kc, qg_ref[g], preferred_element_type=F32)
        s = jnp.where(cmask, s, neg_inf)
        m = jnp.max(s, axis=0, keepdims=True)
        m = jnp.where(m == neg_inf, 0.0, m)
        e = jnp.where(cmask, jnp.exp2(s - m), 0.0)
        pb = (e / jnp.maximum(jnp.sum(e, axis=0, keepdims=True), 1e-30)).astype(BF16)
        r = jnp.dot(v_cov, pb, preferred_element_type=F32)
        o_cmp += [head(r[0:HEAD_DIM], h) for h in range(HPG)]
        imp = head(r[HEAD_DIM:], 0)
        for h in range(1, HPG):
            imp = imp + head(r[HEAD_DIM:], h)
        score = jnp.where(forced, FORCE_SCORE, jnp.where(valid, imp, neg_inf))
        slabs = [score[8 * sj:8 * sj + 8] for sj in range(n_blk // 8)]
        ranks = [jnp.zeros((8, tq), F32) for _ in slabs]
        for jp in range(n_blk):
            other = jnp.broadcast_to(score[jp:jp + 1, :], (8, tq))
            for sj, slab in enumerate(slabs):
                if jp < 8 * sj:
                    inc = jnp.where(other >= slab, 1.0, 0.0)
                elif jp >= 8 * sj + 8:
                    inc = jnp.where(other > slab, 1.0, 0.0)
                else:
                    tie = jnp.where(sub_row > jp - 8 * sj, 1.0, 0.0)
                    inc = jnp.where(other > slab, 1.0, jnp.where(other == slab, tie, 0.0))
                ranks[sj] = ranks[sj] + inc
        bias_ref[g, 0:n_blk, :] = jnp.where(jnp.concatenate(ranks, axis=0) < SLC_TOPN, 0.0, masked)
        bias_ref[g, n_blk:n_blk + 8, :] = jnp.zeros((8, tq), F32)

    n_back = WIN // ck
    n_sel = qi + 1
    n_win = jnp.minimum(qi, n_back) + 1
    n_items = n_sel + n_win
    k_rel = lax.broadcasted_iota(jnp.int32, (ck, tq), 0)
    t_rel = lax.broadcasted_iota(jnp.int32, (ck, tq), 1)
    causal_bias = jnp.where(k_rel <= t_rel, 0.0, neg_inf)
    tail_bias = jnp.where(k_rel > t_rel, 0.0, neg_inf)
    m_ref[...] = jnp.full(m_ref.shape, neg_inf, F32)
    acc_ref[...] = jnp.zeros(acc_ref.shape, F32)

    def item(n):
        br = (n >= n_sel).astype(jnp.int32)
        return br, jnp.minimum(jnp.where(br == 1, qi - n_win + 1 + (n - n_sel), n), qi)

    def key_chunk(n):
        br, ci = item(n)
        return k_ref[0, br, pl.ds(pl.multiple_of(ci * ck, ck), ck), :]

    if bounded:
        k_lane = lax.broadcasted_iota(jnp.int32, (ck, KV_W), 1)
        k_blk = lax.broadcasted_iota(jnp.int32, (ck, KV_W), 0) // SLC_BLK
        own_half = [k_lane < HEAD_DIM, k_lane >= HEAD_DIM]
        spare = [HEAD_DIM, 0]
        k_spare = [jnp.where(k_lane - spare[g] == k_blk, 1.0, jnp.where(k_lane - spare[g] == bpc, 1.0, 0.0)).astype(BF16)
                   for g in range(N_KV)]
        shift_row = jnp.full((1, HPG * tq), -bound_ref[0], F32)
        pad_rows = jnp.zeros((16 - bpc - 1, HPG * tq), F32)

        def set_spare_rows(n):
            br, ci = item(n)
            for g in range(N_KV):
                rows = [tile_h(bias_ref[g, pl.ds(jnp.where(br == 0, ci * bpc + jj, n_blk), 1), :]) for jj in range(bpc)]
                qg_ref[g, spare[g]:spare[g] + 16, :] = jnp.concatenate(rows + [shift_row, pad_rows], axis=0).astype(BF16)

    def scores(k, slot, g, h):
        cols = slice(h * tq, (h + 1) * tq)
        if bounded:
            k = jnp.where(own_half[g], k, k_spare[g])
        s_ref[slot, g, :, cols] = jnp.dot(k, qg_ref[g, :, cols], preferred_element_type=F32)

    def step(n, slot, nxt_slot):
        br, ci = item(n)
        k_next = key_chunk(n + 1)
        if bounded:
            set_spare_rows(n + 1)
        rel = ci - qi
        extra = jnp.where(rel == 0, causal_bias, jnp.where((br == 1) & (rel == -n_back), tail_bias, 0.0))
        extra = jnp.where(n >= n_items, neg_inf, extra)
        for g in range(N_KV):
            v_aug = jnp.concatenate([v_ref[0, br, ci, g * HEAD_DIM:(g + 1) * HEAD_DIM, :], ones_rows], axis=0)
            bias = extra
            if not bounded:
                bias = extra + jnp.concatenate(
                    [jnp.broadcast_to(bias_ref[g, pl.ds(jnp.where(br == 0, ci * bpc + jj, n_blk), 1), :], (SLC_BLK, tq))
                     for jj in range(bpc)], axis=0)
            for h in range(HPG):
                cols = slice(h * tq, (h + 1) * tq)
                scores(k_next, nxt_slot, g, h)
                s = s_ref[slot, g, :, cols] + bias
                if bounded:
                    p = jnp.exp2(s).astype(BF16)
                    acc_ref[br, g, :, cols] += jnp.dot(v_aug, p, preferred_element_type=F32)
                else:
                    m_prev = m_ref[br, g, :, cols]
                    m_new = jnp.maximum(m_prev, jnp.max(s, axis=0, keepdims=True))
                    m_safe = jnp.where(m_new == neg_inf, 0.0, m_new)
                    alpha = jnp.exp2(m_prev - m_safe)
                    p = jnp.exp2(s - m_safe).astype(BF16)
                    acc_ref[br, g, :, cols] = (alpha * acc_ref[br, g, :, cols]
                                               + jnp.dot(v_aug, p, preferred_element_type=F32))
                    m_ref[br, g, :, cols] = m_new

    if bounded:
        set_spare_rows(0)
    k_first = key_chunk(0)
    for g in range(N_KV):
        for h in range(HPG):
            scores(k_first, 0, g, h)

    def body(pair, carry):
        step(2 * pair, 0, 1)
        step(2 * pair + 1, 1, 0)
        return carry

    lax.fori_loop(0, (n_items + 1) // 2, body, 0)

    den_floor = F32_TINY if bounded else 1e-30

    def result(br):
        outs = []
        for g in range(N_KV):
            o = acc_ref[br, g, 0:HEAD_DIM] / jnp.maximum(acc_ref[br, g, HEAD_DIM:HEAD_DIM + 1], den_floor)
            outs += [head(o, h) for h in range(HPG)]
        return outs

    o_slc = result(0)
    o_win = result(1)

    gates = gate_ref[0]
    heads = []
    for idx in range(N_HEADS):
        c = idx * 3
        heads.append(gates[c:c + 1] * o_cmp[idx] + gates[c + 1:c + 2] * o_slc[idx] + gates[c + 2:c + 3] * o_win[idx])
    y = jnp.concatenate(heads, axis=0)
    y = y * lax.rsqrt(jnp.mean(y * y, axis=0, keepdims=True) + EPS) * ow_ref[...]
    y_ref[0] = y.T.astype(BF16)


def _attention(q, kc, vc, k, v, gates, out_norm_w, score_bound):
    B, _, T = q.shape
    ck = v.shape[4]
    tq = ck
    n_rows = kc.shape[1]
    n_cmp = (T - CMP_LEN) // CMP_STRIDE + 1
    nb = T // SLC_BLK
    assert nb <= SLC_BLK and WIN % ck == 0 and T % tq == 0 and ck % SLC_BLK == 0
    cs = np.arange(n_cmp)[None, :] * CMP_STRIDE
    bs = np.arange(nb)[:, None] * SLC_BLK
    cover_t = np.clip(np.minimum(cs + CMP_LEN, bs + SLC_BLK) - np.maximum(cs, bs), 0, None) / CMP_LEN
    cover_t = jnp.asarray(np.pad(cover_t, ((0, SLC_BLK - nb), (0, n_rows - n_cmp))), BF16)

    per_b = lambda *s: pl.BlockSpec((1,) + s, lambda b, i: (b,) + (0,) * len(s))

    def call(bounded):
        return pl.pallas_call(
            functools.partial(_attn_kernel, n_cmp=n_cmp, bounded=bounded),
            grid=(B, T // tq),
            in_specs=[pl.BlockSpec(memory_space=pltpu.SMEM),
                      pl.BlockSpec((1, D_ATTN, tq), lambda b, i: (b, 0, i)),
                      per_b(n_rows, KV_W), per_b(KV_W, n_rows),
                      per_b(2, T, KV_W), per_b(2, T // ck, KV_W, ck),
                      pl.BlockSpec((1, GATE_ROWS, tq), lambda b, i: (b, 0, i)),
                      _full((SLC_BLK, n_rows)), _full((D_ATTN, 1))],
            out_specs=pl.BlockSpec((1, tq, D_ATTN), lambda b, i: (b, i, 0)),
            out_shape=jax.ShapeDtypeStruct((B, T, D_ATTN), BF16),
            scratch_shapes=[pltpu.VMEM((N_KV, KV_W, HPG * tq), BF16),
                            pltpu.VMEM((2, N_KV, ck, HPG * tq), F32),
                            pltpu.VMEM((2, N_KV, HEAD_DIM + 16, HPG * tq), F32),
                            pltpu.VMEM((2, N_KV, 1, HPG * tq), F32),
                            pltpu.VMEM((N_KV, SLC_BLK + 8, tq), F32)],
            compiler_params=_params(2),
            name="nsa_attention_bounded" if bounded else "nsa_attention",
        )(score_bound.reshape(1), q, kc, vc, k, v, gates, cover_t, out_norm_w.reshape(D_ATTN, 1))

    return lax.cond(score_bound <= MAX_SCORE_BOUND, lambda: call(True), lambda: call(False))


def _score_bound(q_norm_w, k_norm_w):
    bf16_slack = (1.0 + 2.0 ** -7) ** 2
    bound = HEAD_DIM * jnp.max(jnp.abs(q_norm_w)) * jnp.max(jnp.abs(k_norm_w)) * (Q_SCALE * bf16_slack)
    return jnp.ceil(bound.astype(F32) * 2.0) * 0.5


def _out_ffn_kernel(x_ref, yr_ref, ya_ref, wo_r_ref, wo_a_ref, fw_ref, wg_ref, wu_ref, wd_ref, o_ref):
    x1 = (x_ref[...] + jnp.dot(yr_ref[...], wo_r_ref[...], preferred_element_type=F32)
          + jnp.dot(ya_ref[...], wo_a_ref[...], preferred_element_type=F32))
    h = (x1 * lax.rsqrt(jnp.mean(x1 * x1, axis=-1, keepdims=True) + EPS) * fw_ref[...]).astype(BF16)
    gate = jnp.dot(h, wg_ref[...], preferred_element_type=F32)
    up = jnp.dot(h, wu_ref[...], preferred_element_type=F32)
    act = (gate * jax.nn.sigmoid(gate) * up).astype(BF16)
    o_ref[...] = x1 + jnp.dot(act, wd_ref[...], preferred_element_type=F32)


def _out_ffn(x, y_rnn, y_att, w_out, ffn_norm_w, w_gate, w_up, w_down):
    B, T, D = x.shape
    n = B * T
    tm = min(FFN_TILE, n)
    d_ff = w_gate.shape[1]
    tok = lambda w: pl.BlockSpec((tm, w), lambda i: (i, 0))
    once = lambda shape: pl.BlockSpec(shape, lambda i: (0, 0), pipeline_mode=pl.Buffered(1))
    out = pl.pallas_call(
        _out_ffn_kernel,
        grid=(n // tm,),
        in_specs=[tok(D), tok(D_RNN), tok(D_ATTN), once((D_RNN, D)), once((D_ATTN, D)), once((1, D)),
                  once((D, d_ff)), once((D, d_ff)), once((d_ff, D))],
        out_specs=tok(D),
        out_shape=jax.ShapeDtypeStruct((n, D), F32),
        compiler_params=_params(1),
        name="out_ffn",
    )(x.reshape(n, D), y_rnn.reshape(n, D_RNN), y_att.reshape(n, D_ATTN),
      w_out[:D_RNN].astype(BF16), w_out[D_RNN:].astype(BF16), ffn_norm_w.reshape(1, D),
      w_gate.astype(BF16), w_up.astype(BF16), w_down.astype(BF16))
    return out.reshape(B, T, D)


def kernel(x, positions, attn_norm_w, w_in, conv_w, conv_b, gate_a_w, gate_a_b, gate_x_w, gate_x_b, lru_lambda, q_norm_w, k_norm_w, cmp_pos, cmp_k_w1, cmp_k_w2, cmp_v_w1, cmp_v_w2, rnn_out_norm_w, attn_out_norm_w, w_out, ffn_norm_w, w_gate, w_up, w_down):
    cos_t, sin_t = _rope_tables(positions)
    for l in range(attn_norm_w.shape[0]):
        xr, gr, kcvc, q, k, v, gates = _in_proj(x, attn_norm_w[l], w_in[l], q_norm_w[l], k_norm_w[l], cos_t, sin_t)
        y_rnn = _rglru(xr, gr, conv_w[l], conv_b[l], gate_a_w[l], gate_a_b[l], gate_x_w[l], gate_x_b[l],
                       lru_lambda[l], rnn_out_norm_w[l])
        kc, vc = _compress(kcvc, cmp_pos[l], cmp_k_w1[l], cmp_k_w2[l], cmp_v_w1[l], cmp_v_w2[l], k_norm_w[l],
                           cos_t, sin_t)
        y_att = _attention(q, kc, vc, k, v, gates, attn_out_norm_w[l], _score_bound(q_norm_w[l], k_norm_w[l]))
        x = _out_ffn(x, y_rnn, y_att, w_out[l], ffn_norm_w[l], w_gate[l], w_up[l], w_down[l])
    return x
```

```python
import functools

import numpy as np
import jax
import jax.numpy as jnp
from jax import lax
from jax.experimental import pallas as pl
from jax.experimental.pallas import tpu as pltpu

F32 = jnp.float32
BF16 = jnp.bfloat16

D_RNN = 512
RNN_BLOCKS = 8
CONV_WIDTH = 4
LRU_C = 8.0
N_HEADS = 8
N_KV = 2
HPG = N_HEADS // N_KV
HEAD_DIM = 64
D_ATTN = N_HEADS * HEAD_DIM
KV_W = N_KV * HEAD_DIM
CMP_LEN = 32
CMP_STRIDE = 16
SLC_BLK = 64
SLC_TOPN = 16
WIN = 512
FORCE_SCORE = 1.0e4
ROPE_THETA = 500000.0
ROPE_DIM = HEAD_DIM // 4
ROPE_HALF = ROPE_DIM // 2
EPS = 1e-6
ATTN_SCALE = HEAD_DIM ** -0.5
LOG2E = 1.4426950408889634
Q_SCALE = ATTN_SCALE * LOG2E
N_GATES = 3 * N_HEADS
GATE_ROWS = 32

MASK_BIG = 1e30
MAX_SCORE_BOUND = 40.0
F32_TINY = 1.1754944e-38

LANES = 128
VMEM_LIMIT = 56 * 1024 * 1024

IN_TILE = 512
RNN_TILE = 256
ATT_TILE = 256
FFN_TILE = 512

_DN_T = (((1,), (1,)), ((), ()))


def _params(n_axes):
    return pltpu.CompilerParams(dimension_semantics=("arbitrary",) * n_axes,
                                vmem_limit_bytes=VMEM_LIMIT)


def _full(shape):
    return pl.BlockSpec(shape, lambda *_: (0,) * len(shape))


def _rope_table_kernel(pos_ref, inv_ref, cos_ref, sin_ref):
    ang = pos_ref[0].astype(F32) * inv_ref[...]
    cos_ref[0] = jnp.cos(ang)
    sin_ref[0] = jnp.sin(ang)


def _rope_tables(positions):
    B, T = positions.shape
    inv = jnp.power(ROPE_THETA, -jnp.arange(ROPE_HALF, dtype=F32) * 2.0 / ROPE_DIM).reshape(ROPE_HALF, 1)
    out = jax.ShapeDtypeStruct((B, ROPE_HALF, T), F32)
    return pl.pallas_call(
        _rope_table_kernel,
        grid=(B,),
        in_specs=[pl.BlockSpec((1, 1, T), lambda b: (b, 0, 0)), _full((ROPE_HALF, 1))],
        out_specs=[pl.BlockSpec((1, ROPE_HALF, T), lambda b: (b, 0, 0))] * 2,
        out_shape=[out, out],
        compiler_params=_params(1),
        name="rope_tables",
    )(positions.reshape(B, 1, T), inv)


def _norm_rope_t(blk, w_col, cos, sin):
    y = blk * lax.rsqrt(jnp.mean(blk * blk, axis=0, keepdims=True) + EPS) * w_col
    x1 = y[0:ROPE_HALF]
    x2 = y[ROPE_HALF:ROPE_DIM]
    return jnp.concatenate([x1 * cos - x2 * sin, x2 * cos + x1 * sin, y[ROPE_DIM:]], axis=0)


def _in_proj_kernel(x_ref, nw_ref, wn_ref, wt_ref, qw_ref, kw_ref, cos_ref, sin_ref,
                    xr_ref, gr_ref, kcvc_ref, q_ref, k_ref, v_ref, gate_ref):
    ck = v_ref.shape[4]
    tm = x_ref.shape[1]
    x = x_ref[0]
    h = x * lax.rsqrt(jnp.mean(x * x, axis=-1, keepdims=True) + EPS) * nw_ref[...]
    hb = h.astype(BF16)
    pn = jnp.dot(hb, wn_ref[...], preferred_element_type=F32)
    pt = lax.dot_general(wt_ref[...], hb, _DN_T, preferred_element_type=F32)
    xr_ref[0] = pn[:, 0:D_RNN]
    gr_ref[0] = pn[:, D_RNN:2 * D_RNN]
    kcvc_ref[0] = pn[:, 2 * D_RNN:2 * D_RNN + 2 * KV_W]
    cos = cos_ref[0]
    sin = sin_ref[0]
    qw = qw_ref[...]
    kw = kw_ref[...]
    for hd in range(N_HEADS):
        rows = slice(hd * HEAD_DIM, (hd + 1) * HEAD_DIM)
        q_ref[0, rows, :] = (_norm_rope_t(pt[rows], qw, cos, sin) * Q_SCALE).astype(BF16)
    for br in range(2):
        base = D_ATTN + br * KV_W
        k_t = jnp.concatenate([_norm_rope_t(pt[base + g * HEAD_DIM:base + (g + 1) * HEAD_DIM], kw, cos, sin)
                               for g in range(N_KV)], axis=0)
        k_ref[0, br] = k_t.T.astype(BF16)
        base = D_ATTN + (2 + br) * KV_W
        for c in range(tm // ck):
            v_ref[0, br, c] = pt[base:base + KV_W, c * ck:(c + 1) * ck].astype(BF16)
    gbase = D_ATTN + 4 * KV_W
    gate_ref[0] = jax.nn.sigmoid(pt[gbase:gbase + GATE_ROWS])


def _in_proj(x, attn_norm_w, w_in, q_norm_w, k_norm_w, cos_t, sin_t):
    B, T, D = x.shape
    tm = min(IN_TILE, T)
    ck = min(ATT_TILE, T)
    sp = np.cumsum((D_RNN, D_RNN, D_ATTN, KV_W, KV_W, KV_W, KV_W, KV_W, KV_W, N_GATES))
    xr_w, gr_w, q_w, kc_w, vc_w, ks_w, vs_w, kw_w, vw_w, gl_w = jnp.split(w_in, sp[:-1], axis=1)
    wn = jnp.concatenate([xr_w, gr_w, kc_w, vc_w], axis=1).astype(BF16)
    wt = jnp.concatenate([q_w, ks_w, kw_w, vs_w, vw_w, gl_w,
                          jnp.zeros((D, GATE_ROWS - N_GATES), w_in.dtype)], axis=1).T.astype(BF16)
    n_wn = wn.shape[1]
    n_wt = wt.shape[0]
    f32o = lambda *s: jax.ShapeDtypeStruct(s, F32)
    bfo = lambda *s: jax.ShapeDtypeStruct(s, BF16)
    tok = lambda w: pl.BlockSpec((1, tm, w), lambda b, i: (b, i, 0))
    feat = lambda r: pl.BlockSpec((1, r, tm), lambda b, i: (b, 0, i))
    keys = pl.BlockSpec((1, 2, tm, KV_W), lambda b, i: (b, 0, i, 0))
    vals = pl.BlockSpec((1, 2, tm // ck, KV_W, ck), lambda b, i: (b, 0, i, 0, 0))
    return pl.pallas_call(
        _in_proj_kernel,
        grid=(B, T // tm),
        in_specs=[tok(D), _full((1, D)), _full((D, n_wn)), _full((n_wt, D)),
                  _full((HEAD_DIM, 1)), _full((HEAD_DIM, 1)), feat(ROPE_HALF), feat(ROPE_HALF)],
        out_specs=[tok(D_RNN), tok(D_RNN), tok(2 * KV_W), feat(D_ATTN), keys, vals, feat(GATE_ROWS)],
        out_shape=[f32o(B, T, D_RNN), f32o(B, T, D_RNN), f32o(B, T, 2 * KV_W),
                   bfo(B, D_ATTN, T), bfo(B, 2, T, KV_W), bfo(B, 2, T // ck, KV_W, ck),
                   f32o(B, GATE_ROWS, T)],
        compiler_params=_params(2),
        name="in_proj",
    )(x, attn_norm_w.reshape(1, D), wn, wt, q_norm_w.reshape(HEAD_DIM, 1), k_norm_w.reshape(HEAD_DIM, 1),
      cos_t, sin_t)


def _rglru_kernel(xr_ref, gr_ref, cw_ref, cb_ref, wa_ref, ba_ref, wx_ref, bx_ref, lam_ref, ow_ref,
                  y_ref, ext_ref, h_ref):
    tt = xr_ref.shape[1]
    pad = 8

    @pl.when(pl.program_id(1) == 0)
    def _():
        ext_ref[0:pad, :] = jnp.zeros((pad, D_RNN), F32)
        h_ref[...] = jnp.zeros((1, D_RNN), F32)

    xr = xr_ref[0]
    ext_ref[pad:pad + tt, :] = xr
    xc = cb_ref[...]
    for k in range(CONV_WIDTH):
        off = pad - (CONV_WIDTH - 1) + k
        xc = xc + cw_ref[k:k + 1, :] * ext_ref[off:off + tt, :]
    ext_ref[0:pad, :] = xr[tt - pad:tt]
    xcb = xc.astype(BF16)
    r = jax.nn.sigmoid(jnp.dot(xcb, wa_ref[...], preferred_element_type=F32) + ba_ref[...])
    i = jax.nn.sigmoid(jnp.dot(xcb, wx_ref[...], preferred_element_type=F32) + bx_ref[...])
    z = -lam_ref[...]
    softplus = jnp.maximum(z, 0.0) + jnp.log1p(jnp.exp(-jnp.abs(z)))
    log_a = -LRU_C * r * softplus
    a = jnp.exp(log_a)
    t2 = 2.0 * log_a
    u = jnp.exp(t2)
    neg_expm1 = jnp.where(u == 1.0, -t2, (1.0 - u) * t2 / jnp.log(u))
    b = jnp.sqrt(neg_expm1) * (i * xc)
    row = lax.broadcasted_iota(jnp.int32, (tt, D_RNN), 0) & 7
    for s in (1, 2, 4):
        keep = row >= s
        a_sh = jnp.where(keep, pltpu.roll(a, s, 0), 1.0)
        b_sh = jnp.where(keep, pltpu.roll(b, s, 0), 0.0)
        b = a * b_sh + b
        a = a * a_sh
    h = h_ref[...]
    hs = []
    for g in range(tt // 8):
        hg = a[g * 8:(g + 1) * 8] * h + b[g * 8:(g + 1) * 8]
        h = hg[7:8]
        hs.append(hg)
    h_ref[...] = h
    y = jax.nn.gelu(gr_ref[0]) * jnp.concatenate(hs, axis=0)
    y = y * lax.rsqrt(jnp.mean(y * y, axis=-1, keepdims=True) + EPS) * ow_ref[...]
    y_ref[0] = y.astype(BF16)


def _block_diag(w):
    n, d, _ = w.shape
    eye = jnp.eye(n, dtype=w.dtype)
    return (eye[:, None, :, None] * w[:, :, None, :]).reshape(n * d, n * d)


def _rglru(xr, gr, conv_w, conv_b, gate_a_w, gate_a_b, gate_x_w, gate_x_b, lru_lambda, out_norm_w):
    B, T, _ = xr.shape
    tt = min(RNN_TILE, T)
    tok = pl.BlockSpec((1, tt, D_RNN), lambda b, i: (b, i, 0))
    row = lambda a: a.reshape(1, D_RNN)
    return pl.pallas_call(
        _rglru_kernel,
        grid=(B, T // tt),
        in_specs=[tok, tok, _full((CONV_WIDTH, D_RNN)), _full((1, D_RNN)),
                  _full((D_RNN, D_RNN)), _full((1, D_RNN)), _full((D_RNN, D_RNN)), _full((1, D_RNN)),
                  _full((1, D_RNN)), _full((1, D_RNN))],
        out_specs=tok,
        out_shape=jax.ShapeDtypeStruct((B, T, D_RNN), BF16),
        scratch_shapes=[pltpu.VMEM((tt + 8, D_RNN), F32), pltpu.VMEM((1, D_RNN), F32)],
        compiler_params=_params(2),
        name="rglru",
    )(xr, gr, conv_w, row(conv_b), _block_diag(gate_a_w).astype(BF16), row(gate_a_b),
      _block_diag(gate_x_w).astype(BF16), row(gate_x_b), row(lru_lambda), row(out_norm_w))


def _compress_kernel(rk_ref, rv_ref, pos_ref, w1k_ref, w2k_ref, w1v_ref, w2v_ref, kw_ref, cos_ref, sin_ref,
                     kc_ref, vc_ref):
    half = pos_ref.shape[1]
    n_rows = rk_ref.shape[2]

    def hidden(r, w1_ref):
        top = jnp.dot((r + pos_ref[0:1, :]).astype(BF16), w1_ref[0:half, :], preferred_element_type=F32)
        bot = jnp.dot((r + pos_ref[1:2, :]).astype(BF16), w1_ref[half:2 * half, :], preferred_element_type=F32)
        return jax.nn.gelu(top + pltpu.roll(bot, n_rows - 1, 0)).astype(BF16)

    kc_t = []
    vc_t = []
    for g in range(N_KV):
        k_t = lax.dot_general(w2k_ref[...], hidden(rk_ref[0, g], w1k_ref), _DN_T, preferred_element_type=F32)
        kc_t.append(_norm_rope_t(k_t, kw_ref[...], cos_ref[0], sin_ref[0]))
        vc_t.append(lax.dot_general(w2v_ref[...], hidden(rv_ref[0, g], w1v_ref), _DN_T,
                                    preferred_element_type=F32))
    kc_ref[0] = jnp.concatenate(kc_t, axis=0).T.astype(BF16)
    vc_ref[0] = jnp.concatenate(vc_t, axis=0).astype(BF16)


def _compress(kcvc, cmp_pos, k_w1, k_w2, v_w1, v_w2, k_norm_w, cos_t, sin_t):
    B, T, _ = kcvc.shape
    n_rows = T // CMP_STRIDE
    n_cmp = (T - CMP_LEN) // CMP_STRIDE + 1
    half = CMP_STRIDE * HEAD_DIM
    hid = k_w1.shape[1]

    def rows(tokens):
        return tokens.reshape(B, T, N_KV, HEAD_DIM).transpose(0, 2, 1, 3).reshape(B, N_KV, n_rows, half)

    def at_block_end(tab):
        tab = tab[:, :, CMP_LEN - 1::CMP_STRIDE]
        return jnp.pad(tab, ((0, 0), (0, 0), (0, n_rows - n_cmp)))

    blk = pl.BlockSpec((1, N_KV, n_rows, half), lambda b: (b, 0, 0, 0))
    rope = pl.BlockSpec((1, ROPE_HALF, n_rows), lambda b: (b, 0, 0))
    return pl.pallas_call(
        _compress_kernel,
        grid=(B,),
        in_specs=[blk, blk, _full((2, half)), _full((2 * half, hid)), _full((HEAD_DIM, hid)),
                  _full((2 * half, hid)), _full((HEAD_DIM, hid)), _full((HEAD_DIM, 1)), rope, rope],
        out_specs=[pl.BlockSpec((1, n_rows, KV_W), lambda b: (b, 0, 0)),
                   pl.BlockSpec((1, KV_W, n_rows), lambda b: (b, 0, 0))],
        out_shape=[jax.ShapeDtypeStruct((B, n_rows, KV_W), BF16), jax.ShapeDtypeStruct((B, KV_W, n_rows), BF16)],
        compiler_params=_params(1),
        name="compress",
    )(rows(kcvc[:, :, :KV_W]), rows(kcvc[:, :, KV_W:]), cmp_pos.reshape(2, half),
      k_w1.astype(BF16), k_w2.T.astype(BF16), v_w1.astype(BF16), v_w2.T.astype(BF16),
      k_norm_w.reshape(HEAD_DIM, 1), at_block_end(cos_t), at_block_end(sin_t))


def _attn_kernel(bound_ref, q_ref, kc_ref, vc_ref, k_ref, v_ref, gate_ref, cover_ref, ow_ref,
                 y_ref, qg_ref, s_ref, acc_ref, m_ref, bias_ref, *, n_cmp, bounded):
    tq = q_ref.shape[2]
    ck = v_ref.shape[4]
    n_rows = kc_ref.shape[1]
    n_blk = SLC_BLK
    bpc = ck // SLC_BLK
    n_back = WIN // ck
    l_rows = 16
    sel, win = 0, 1
    qi = pl.program_id(1)
    q0 = qi * tq
    neg_inf = -jnp.inf
    masked = -MASK_BIG if bounded else neg_inf
    den_floor = F32_TINY if bounded else 1e-30
    t_row = q0 + lax.broadcasted_iota(jnp.int32, (1, tq), 1)
    tile_h = lambda a: jnp.concatenate([a] * HPG, axis=1)
    col = lambda h: slice(h * tq, (h + 1) * tq)

    zero_half = jnp.zeros((HEAD_DIM, HPG * tq), BF16)
    for g in range(N_KV):
        qg = jnp.concatenate([q_ref[0, (g * HPG + h) * HEAD_DIM:(g * HPG + h + 1) * HEAD_DIM, :]
                              for h in range(HPG)], axis=1)
        qg_ref[g] = jnp.concatenate([qg, zero_half] if g == 0 else [zero_half, qg], axis=0)
    m_ref[...] = jnp.full(m_ref.shape, neg_inf, F32)
    acc_ref[...] = jnp.zeros(acc_ref.shape, F32)

    k_rel = lax.broadcasted_iota(jnp.int32, (ck, tq), 0)
    t_rel = lax.broadcasted_iota(jnp.int32, (ck, tq), 1)
    causal_bias = jnp.where(k_rel <= t_rel, 0.0, neg_inf)
    tail_bias = jnp.where(k_rel > t_rel, 0.0, neg_inf)
    ones_rows = jnp.ones((l_rows, ck), BF16)

    def half_mask(rows, g):
        lane = lax.broadcasted_iota(jnp.int32, (rows, KV_W), 1)
        return lane < HEAD_DIM if g == 0 else lane >= HEAD_DIM

    if bounded:
        spare = [HEAD_DIM, 0]
        k_lane = lax.broadcasted_iota(jnp.int32, (ck, KV_W), 1)
        k_blk = lax.broadcasted_iota(jnp.int32, (ck, KV_W), 0) // SLC_BLK
        own_half = [half_mask(ck, g) for g in range(N_KV)]
        k_spare = [jnp.where(k_lane - spare[g] == k_blk, 1.0, jnp.where(k_lane - spare[g] == bpc, 1.0, 0.0)).astype(BF16)
                   for g in range(N_KV)]
        shift_row = jnp.full((1, HPG * tq), -bound_ref[0], F32)
        pad_rows = jnp.zeros((16 - bpc - 1, HPG * tq), F32)
        zero_row = jnp.zeros((1, HPG * tq), F32)

        def set_spare_rows(g, bias_rows):
            qg_ref[g, spare[g]:spare[g] + 16, :] = jnp.concatenate(bias_rows + [shift_row, pad_rows], axis=0).astype(BF16)

    def key_chunk(br, ci):
        return k_ref[0, br, pl.ds(pl.multiple_of(ci * ck, ck), ck), :]

    def value_chunk(br, ci, g):
        return jnp.concatenate([v_ref[0, br, ci, g * HEAD_DIM:(g + 1) * HEAD_DIM, :], ones_rows], axis=0)

    def scores(k, slot, g, h):
        if bounded:
            k = jnp.where(own_half[g], k, k_spare[g])
        s_ref[slot, g, :, col(h)] = jnp.dot(k, qg_ref[g, :, col(h)], preferred_element_type=F32)

    def softmax_pv(br, slot, g, h, v_aug, bias):
        s = s_ref[slot, g, :, col(h)]
        if bias is not None:
            s = s + bias
        if bounded:
            p = jnp.exp2(s).astype(BF16)
            acc_ref[br, g, :, col(h)] += jnp.dot(v_aug, p, preferred_element_type=F32)
        else:
            m_prev = m_ref[br, g, :, col(h)]
            m_new = jnp.maximum(m_prev, jnp.max(s, axis=0, keepdims=True))
            m_safe = jnp.where(m_new == neg_inf, 0.0, m_new)
            alpha = jnp.exp2(m_prev - m_safe)
            p = jnp.exp2(s - m_safe).astype(BF16)
            acc_ref[br, g, :, col(h)] = (alpha * acc_ref[br, g, :, col(h)]
                                         + jnp.dot(v_aug, p, preferred_element_type=F32))
            m_ref[br, g, :, col(h)] = m_new

    def block_bias_rows(g, ci):
        return [bias_ref[g, pl.ds(ci * bpc + jj, 1), :] for jj in range(bpc)]

    def window_chunk(w, next_qk):
        ci = qi - n_back + w
        bias = tail_bias if w == 0 else causal_bias if w == n_back else None
        if w < n_back:
            bias = jnp.where(ci >= 0, 0.0 if bias is None else bias, neg_inf)
        for g in range(N_KV):
            v_aug = value_chunk(win, jnp.maximum(ci, 0), g)
            for h in range(HPG):
                next_qk(g, h)
                softmax_pv(win, w % 2, g, h, v_aug, bias)
                yield

    def window_early():
        if bounded:
            for g in range(N_KV):
                set_spare_rows(g, [zero_row] * bpc)
        k0 = key_chunk(win, jnp.maximum(qi - n_back, 0))
        for g in range(N_KV):
            for h in range(HPG):
                scores(k0, 0, g, h)
                yield
        for w in range(n_back):
            k_next = key_chunk(win, jnp.maximum(qi - n_back + w + 1, 0))
            yield from window_chunk(w, lambda g, h, k=k_next, slot=(w + 1) % 2: scores(k, slot, g, h))

    c_col = lax.broadcasted_iota(jnp.int32, (n_rows, tq), 0)
    cmask = ((c_col * CMP_STRIDE + (CMP_LEN - 1)) <= t_row) & (c_col < n_cmp)
    j_col = lax.broadcasted_iota(jnp.int32, (n_blk, tq), 0)
    cur = t_row // SLC_BLK
    forced = (j_col == 0) | (j_col == cur) | (j_col == cur - 1)
    valid = (j_col * SLC_BLK) <= t_row
    sub_row = lax.broadcasted_iota(jnp.int32, (8, tq), 0)
    kc = kc_ref[0]
    o_cmp = [None] * N_HEADS
    imp = [None] * N_KV
    if bounded:
        cmp_shift = jnp.where(cmask, -bound_ref[0], neg_inf)
        cmp_ones = jnp.ones((l_rows, n_rows), BF16)

    def compressed(g):
        v_cov = [vc_ref[0, g * HEAD_DIM:(g + 1) * HEAD_DIM, :], cover_ref[...]]
        if bounded:
            v_cov = jnp.concatenate(v_cov + [cmp_ones], axis=0)
            kc_g = jnp.where(half_mask(n_rows, g), kc, jnp.zeros_like(kc))
        else:
            v_cov = jnp.concatenate(v_cov, axis=0)
            kc_g = kc
        for h in range(HPG):
            s = jnp.dot(kc_g, qg_ref[g, :, col(h)], preferred_element_type=F32)
            if bounded:
                r = jnp.dot(v_cov, jnp.exp2(s + cmp_shift).astype(BF16), preferred_element_type=F32)
                r = r[0:HEAD_DIM + n_blk] / jnp.maximum(r[HEAD_DIM + n_blk:HEAD_DIM + n_blk + 1], den_floor)
            else:
                s = jnp.where(cmask, s, neg_inf)
                m = jnp.max(s, axis=0, keepdims=True)
                m = jnp.where(m == neg_inf, 0.0, m)
                e = jnp.where(cmask, jnp.exp2(s - m), 0.0)
                pb = (e / jnp.maximum(jnp.sum(e, axis=0, keepdims=True), den_floor)).astype(BF16)
                r = jnp.dot(v_cov, pb, preferred_element_type=F32)
            o_cmp[g * HPG + h] = r[0:HEAD_DIM]
            imp[g] = r[HEAD_DIM:] if imp[g] is None else imp[g] + r[HEAD_DIM:]
            yield

    def select(g):
        score = jnp.where(forced, FORCE_SCORE, jnp.where(valid, imp[g], neg_inf))
        slabs = [score[8 * sj:8 * sj + 8] for sj in range(n_blk // 8)]
        ranks = [jnp.zeros((8, tq), F32) for _ in slabs]
        for jp in range(n_blk):
            other = jnp.broadcast_to(score[jp:jp + 1, :], (8, tq))
            for sj, slab in enumerate(slabs):
                if jp < 8 * sj:
                    inc = jnp.where(other >= slab, 1.0, 0.0)
                elif jp >= 8 * sj + 8:
                    inc = jnp.where(other > slab, 1.0, 0.0)
                else:
                    tie = jnp.where(sub_row > jp - 8 * sj, 1.0, 0.0)
                    inc = jnp.where(other > slab, 1.0, jnp.where(other == slab, tie, 0.0))
                ranks[sj] = ranks[sj] + inc
            if jp % 4 == 3:
                yield
        bias_ref[g] = jnp.where(jnp.concatenate(ranks, axis=0) < SLC_TOPN, 0.0, masked)

    def interleave(main, filler, share):
        due = 0.0
        for _ in main:
            due += share
            while due >= 1.0:
                due -= 1.0
                next(filler, None)

    early = window_early()
    n_early = N_HEADS * (n_back + 1)
    n_main = N_KV * (HPG + n_blk // 4)
    for g in range(N_KV):
        interleave(compressed(g), early, n_early / n_main)
        interleave(select(g), early, n_early / n_main)
    for _ in early:
        pass

    first_slot = (n_back + 1) % 2
    if bounded:
        for g in range(N_KV):
            set_spare_rows(g, [tile_h(r) for r in block_bias_rows(g, 0)])
    k_first = key_chunk(sel, 0)
    for _ in window_chunk(n_back, lambda g, h: scores(k_first, first_slot, g, h)):
        pass

    n_sel = qi + 1

    def selected_chunk(n, slot):
        ci = jnp.minimum(n, qi)
        ci_next = jnp.minimum(n + 1, qi)
        k_next = key_chunk(sel, ci_next)
        extra = jnp.where(n >= n_sel, neg_inf, jnp.where(n == qi, causal_bias, 0.0))
        for g in range(N_KV):
            v_aug = value_chunk(sel, ci, g)
            if bounded:
                set_spare_rows(g, [tile_h(r) for r in block_bias_rows(g, ci_next)])
                bias = extra
            else:
                bias = extra + jnp.concatenate([jnp.broadcast_to(r, (SLC_BLK, tq)) for r in block_bias_rows(g, ci)],
                                               axis=0)
            for h in range(HPG):
                scores(k_next, 1 - slot, g, h)
                softmax_pv(sel, slot, g, h, v_aug, bias)

    def body(pair, carry):
        selected_chunk(2 * pair, first_slot)
        selected_chunk(2 * pair + 1, 1 - first_slot)
        return carry

    lax.fori_loop(0, (n_sel + 1) // 2, body, 0)

    def result(br):
        outs = []
        for g in range(N_KV):
            o = acc_ref[br, g, 0:HEAD_DIM] / jnp.maximum(acc_ref[br, g, HEAD_DIM:HEAD_DIM + 1], den_floor)
            outs += [o[:, col(h)] for h in range(HPG)]
        return outs

    o_slc = result(sel)
    o_win = result(win)

    gates = gate_ref[0]
    heads = []
    for idx in range(N_HEADS):
        c = idx * 3
        heads.append(gates[c:c + 1] * o_cmp[idx] + gates[c + 1:c + 2] * o_slc[idx] + gates[c + 2:c + 3] * o_win[idx])
    y = jnp.concatenate(heads, axis=0)
    y = y * lax.rsqrt(jnp.mean(y * y, axis=0, keepdims=True) + EPS) * ow_ref[...]
    y_ref[0] = y.T.astype(BF16)


def _attention(q, kc, vc, k, v, gates, out_norm_w, score_bound):
    B, _, T = q.shape
    ck = v.shape[4]
    tq = ck
    n_rows = kc.shape[1]
    n_cmp = (T - CMP_LEN) // CMP_STRIDE + 1
    nb = T // SLC_BLK
    assert nb <= SLC_BLK and WIN % ck == 0 and T % tq == 0 and ck % SLC_BLK == 0
    cs = np.arange(n_cmp)[None, :] * CMP_STRIDE
    bs = np.arange(nb)[:, None] * SLC_BLK
    cover_t = np.clip(np.minimum(cs + CMP_LEN, bs + SLC_BLK) - np.maximum(cs, bs), 0, None) / CMP_LEN
    cover_t = jnp.asarray(np.pad(cover_t, ((0, SLC_BLK - nb), (0, n_rows - n_cmp))), BF16)

    per_b = lambda *s: pl.BlockSpec((1,) + s, lambda b, i: (b,) + (0,) * len(s))

    def call(bounded):
        return pl.pallas_call(
            functools.partial(_attn_kernel, n_cmp=n_cmp, bounded=bounded),
            grid=(B, T // tq),
            in_specs=[pl.BlockSpec(memory_space=pltpu.SMEM),
                      pl.BlockSpec((1, D_ATTN, tq), lambda b, i: (b, 0, i)),
                      per_b(n_rows, KV_W), per_b(KV_W, n_rows),
                      per_b(2, T, KV_W), per_b(2, T // ck, KV_W, ck),
                      pl.BlockSpec((1, GATE_ROWS, tq), lambda b, i: (b, 0, i)),
                      _full((SLC_BLK, n_rows)), _full((D_ATTN, 1))],
            out_specs=pl.BlockSpec((1, tq, D_ATTN), lambda b, i: (b, i, 0)),
            out_shape=jax.ShapeDtypeStruct((B, T, D_ATTN), BF16),
            scratch_shapes=[pltpu.VMEM((N_KV, KV_W, HPG * tq), BF16),
                            pltpu.VMEM((2, N_KV, ck, HPG * tq), F32),
                            pltpu.VMEM((2, N_KV, HEAD_DIM + 16, HPG * tq), F32),
                            pltpu.VMEM((2, N_KV, 1, HPG * tq), F32),
                            pltpu.VMEM((N_KV, SLC_BLK, tq), F32)],
            compiler_params=_params(2),
            name="nsa_attention_bounded" if bounded else "nsa_attention",
        )(score_bound.reshape(1), q, kc, vc, k, v, gates, cover_t, out_norm_w.reshape(D_ATTN, 1))

    return lax.cond(score_bound <= MAX_SCORE_BOUND, lambda: call(True), lambda: call(False))


def _score_bound(q_norm_w, k_norm_w):
    bf16_slack = (1.0 + 2.0 ** -7) ** 2
    bound = HEAD_DIM * jnp.max(jnp.abs(q_norm_w)) * jnp.max(jnp.abs(k_norm_w)) * (Q_SCALE * bf16_slack)
    return jnp.ceil(bound.astype(F32) * 2.0) * 0.5


def _out_ffn_kernel(x_ref, yr_ref, ya_ref, wo_r_ref, wo_a_ref, fw_ref, wg_ref, wu_ref, wd_ref, o_ref):
    x1 = (x_ref[...] + jnp.dot(yr_ref[...], wo_r_ref[...], preferred_element_type=F32)
          + jnp.dot(ya_ref[...], wo_a_ref[...], preferred_element_type=F32))
    h = (x1 * lax.rsqrt(jnp.mean(x1 * x1, axis=-1, keepdims=True) + EPS) * fw_ref[...]).astype(BF16)
    gate = jnp.dot(h, wg_ref[...], preferred_element_type=F32)
    up = jnp.dot(h, wu_ref[...], preferred_element_type=F32)
    act = (gate * jax.nn.sigmoid(gate) * up).astype(BF16)
    o_ref[...] = x1 + jnp.dot(act, wd_ref[...], preferred_element_type=F32)


def _out_ffn(x, y_rnn, y_att, w_out, ffn_norm_w, w_gate, w_up, w_down):
    B, T, D = x.shape
    n = B * T
    tm = min(FFN_TILE, n)
    d_ff = w_gate.shape[1]
    tok = lambda w: pl.BlockSpec((tm, w), lambda i: (i, 0))
    once = lambda shape: pl.BlockSpec(shape, lambda i: (0, 0), pipeline_mode=pl.Buffered(1))
    out = pl.pallas_call(
        _out_ffn_kernel,
        grid=(n // tm,),
        in_specs=[tok(D), tok(D_RNN), tok(D_ATTN), once((D_RNN, D)), once((D_ATTN, D)), once((1, D)),
                  once((D, d_ff)), once((D, d_ff)), once((d_ff, D))],
        out_specs=tok(D),
        out_shape=jax.ShapeDtypeStruct((n, D), F32),
        compiler_params=_params(1),
        name="out_ffn",
    )(x.reshape(n, D), y_rnn.reshape(n, D_RNN), y_att.reshape(n, D_ATTN),
      w_out[:D_RNN].astype(BF16), w_out[D_RNN:].astype(BF16), ffn_norm_w.reshape(1, D),
      w_gate.astype(BF16), w_up.astype(BF16), w_down.astype(BF16))
    return out.reshape(B, T, D)


def kernel(x, positions, attn_norm_w, w_in, conv_w, conv_b, gate_a_w, gate_a_b, gate_x_w, gate_x_b, lru_lambda, q_norm_w, k_norm_w, cmp_pos, cmp_k_w1, cmp_k_w2, cmp_v_w1, cmp_v_w2, rnn_out_norm_w, attn_out_norm_w, w_out, ffn_norm_w, w_gate, w_up, w_down):
    cos_t, sin_t = _rope_tables(positions)
    for l in range(attn_norm_w.shape[0]):
        xr, gr, kcvc, q, k, v, gates = _in_proj(x, attn_norm_w[l], w_in[l], q_norm_w[l], k_norm_w[l], cos_t, sin_t)
        y_rnn = _rglru(xr, gr, conv_w[l], conv_b[l], gate_a_w[l], gate_a_b[l], gate_x_w[l], gate_x_b[l],
                       lru_lambda[l], rnn_out_norm_w[l])
        kc, vc = _compress(kcvc, cmp_pos[l], cmp_k_w1[l], cmp_k_w2[l], cmp_v_w1[l], cmp_v_w2[l], k_norm_w[l],
                           cos_t, sin_t)
        y_att = _attention(q, kc, vc, k, v, gates, attn_out_norm_w[l], _score_bound(q_norm_w[l], k_norm_w[l]))
        x = _out_ffn(x, y_rnn, y_att, w_out[l], ffn_norm_w[l], w_gate[l], w_up[l], w_down[l])
    return x
```

```python
import functools

import numpy as np
import jax
import jax.numpy as jnp
from jax import lax
from jax.experimental import pallas as pl
from jax.experimental.pallas import tpu as pltpu

F32 = jnp.float32
BF16 = jnp.bfloat16

D_RNN = 512
RNN_BLOCKS = 8
CONV_WIDTH = 4
LRU_C = 8.0
N_HEADS = 8
N_KV = 2
HPG = N_HEADS // N_KV
HEAD_DIM = 64
D_ATTN = N_HEADS * HEAD_DIM
KV_W = N_KV * HEAD_DIM
CMP_LEN = 32
CMP_STRIDE = 16
SLC_BLK = 64
SLC_TOPN = 16
WIN = 512
FORCE_SCORE = 1.0e4
ROPE_THETA = 500000.0
ROPE_DIM = HEAD_DIM // 4
ROPE_HALF = ROPE_DIM // 2
EPS = 1e-6
ATTN_SCALE = HEAD_DIM ** -0.5
LOG2E = 1.4426950408889634
Q_SCALE = ATTN_SCALE * LOG2E
N_GATES = 3 * N_HEADS
GATE_ROWS = 32

MASK_BIG = 1e30
MAX_SCORE_BOUND = 40.0
F32_TINY = 1.1754944e-38

LANES = 128
VMEM_LIMIT = 56 * 1024 * 1024

IN_TILE = 512
ATT_TILE = 256
FFN_TILE = 512
FFN_CHUNK = 256
MIXER_ROWS = 128

_DN_T = (((1,), (1,)), ((), ()))


def _params(n_axes):
    return pltpu.CompilerParams(dimension_semantics=("arbitrary",) * n_axes,
                                vmem_limit_bytes=VMEM_LIMIT)


def _full(shape):
    return pl.BlockSpec(shape, lambda *_: (0,) * len(shape))


def _rope_table_kernel(pos_ref, inv_ref, cos_ref, sin_ref):
    ang = pos_ref[0].astype(F32) * inv_ref[...]
    cos_ref[0] = jnp.cos(ang)
    sin_ref[0] = jnp.sin(ang)


def _rope_tables(positions):
    B, T = positions.shape
    inv = jnp.power(ROPE_THETA, -jnp.arange(ROPE_HALF, dtype=F32) * 2.0 / ROPE_DIM).reshape(ROPE_HALF, 1)
    out = jax.ShapeDtypeStruct((B, ROPE_HALF, T), F32)
    return pl.pallas_call(
        _rope_table_kernel,
        grid=(B,),
        in_specs=[pl.BlockSpec((1, 1, T), lambda b: (b, 0, 0)), _full((ROPE_HALF, 1))],
        out_specs=[pl.BlockSpec((1, ROPE_HALF, T), lambda b: (b, 0, 0))] * 2,
        out_shape=[out, out],
        compiler_params=_params(1),
        name="rope_tables",
    )(positions.reshape(B, 1, T), inv)


def _norm_rope_t(blk, w_col, cos, sin):
    y = blk * lax.rsqrt(jnp.mean(blk * blk, axis=0, keepdims=True) + EPS) * w_col
    x1 = y[0:ROPE_HALF]
    x2 = y[ROPE_HALF:ROPE_DIM]
    return jnp.concatenate([x1 * cos - x2 * sin, x2 * cos + x1 * sin, y[ROPE_DIM:]], axis=0)


def _in_proj_kernel(x_ref, nw_ref, wn_ref, wt_ref, qw_ref, kw_ref, cos_ref, sin_ref,
                    xr_ref, gr_ref, kcvc_ref, q_ref, k_ref, v_ref, gate_ref):
    ck = v_ref.shape[4]
    tm = x_ref.shape[1]
    x = x_ref[0]
    h = x * lax.rsqrt(jnp.mean(x * x, axis=-1, keepdims=True) + EPS) * nw_ref[...]
    hb = h.astype(BF16)
    pn = jnp.dot(hb, wn_ref[...], preferred_element_type=F32)
    pt = lax.dot_general(wt_ref[...], hb, _DN_T, preferred_element_type=F32)
    xr_ref[0] = pn[:, 0:D_RNN]
    gr_ref[0] = pn[:, D_RNN:2 * D_RNN]
    kcvc_ref[0] = pn[:, 2 * D_RNN:2 * D_RNN + 2 * KV_W]
    cos = cos_ref[0]
    sin = sin_ref[0]
    qw = qw_ref[...]
    kw = kw_ref[...]
    for hd in range(N_HEADS):
        rows = slice(hd * HEAD_DIM, (hd + 1) * HEAD_DIM)
        q_ref[0, rows, :] = (_norm_rope_t(pt[rows], qw, cos, sin) * Q_SCALE).astype(BF16)
    for br in range(2):
        base = D_ATTN + br * KV_W
        k_t = jnp.concatenate([_norm_rope_t(pt[base + g * HEAD_DIM:base + (g + 1) * HEAD_DIM], kw, cos, sin)
                               for g in range(N_KV)], axis=0)
        k_ref[0, br] = k_t.T.astype(BF16)
        base = D_ATTN + (2 + br) * KV_W
        for c in range(tm // ck):
            v_ref[0, br, c] = pt[base:base + KV_W, c * ck:(c + 1) * ck].astype(BF16)
    gbase = D_ATTN + 4 * KV_W
    gate_ref[0] = jax.nn.sigmoid(pt[gbase:gbase + GATE_ROWS])


def _in_proj(x, attn_norm_w, w_in, q_norm_w, k_norm_w, cos_t, sin_t):
    B, T, D = x.shape
    tm = min(IN_TILE, T)
    ck = min(ATT_TILE, T)
    sp = np.cumsum((D_RNN, D_RNN, D_ATTN, KV_W, KV_W, KV_W, KV_W, KV_W, KV_W, N_GATES))
    xr_w, gr_w, q_w, kc_w, vc_w, ks_w, vs_w, kw_w, vw_w, gl_w = jnp.split(w_in, sp[:-1], axis=1)
    wn = jnp.concatenate([xr_w, gr_w, kc_w, vc_w], axis=1).astype(BF16)
    wt = jnp.concatenate([q_w, ks_w, kw_w, vs_w, vw_w, gl_w,
                          jnp.zeros((D, GATE_ROWS - N_GATES), w_in.dtype)], axis=1).T.astype(BF16)
    n_wn = wn.shape[1]
    n_wt = wt.shape[0]
    f32o = lambda *s: jax.ShapeDtypeStruct(s, F32)
    bfo = lambda *s: jax.ShapeDtypeStruct(s, BF16)
    tok = lambda w: pl.BlockSpec((1, tm, w), lambda b, i: (b, i, 0))
    feat = lambda r: pl.BlockSpec((1, r, tm), lambda b, i: (b, 0, i))
    keys = pl.BlockSpec((1, 2, tm, KV_W), lambda b, i: (b, 0, i, 0))
    vals = pl.BlockSpec((1, 2, tm // ck, KV_W, ck), lambda b, i: (b, 0, i, 0, 0))
    return pl.pallas_call(
        _in_proj_kernel,
        grid=(B, T // tm),
        in_specs=[tok(D), _full((1, D)), _full((D, n_wn)), _full((n_wt, D)),
                  _full((HEAD_DIM, 1)), _full((HEAD_DIM, 1)), feat(ROPE_HALF), feat(ROPE_HALF)],
        out_specs=[tok(D_RNN), tok(D_RNN), tok(2 * KV_W), feat(D_ATTN), keys, vals, feat(GATE_ROWS)],
        out_shape=[f32o(B, T, D_RNN), f32o(B, T, D_RNN), f32o(B, T, 2 * KV_W),
                   bfo(B, D_ATTN, T), bfo(B, 2, T, KV_W), bfo(B, 2, T // ck, KV_W, ck),
                   f32o(B, GATE_ROWS, T)],
        compiler_params=_params(2),
        name="in_proj",
    )(x, attn_norm_w.reshape(1, D), wn, wt, q_norm_w.reshape(HEAD_DIM, 1), k_norm_w.reshape(HEAD_DIM, 1),
      cos_t, sin_t)


def _compress_kernel(rk_ref, rv_ref, pos_ref, w1k_ref, w2k_ref, w1v_ref, w2v_ref, kw_ref, cos_ref, sin_ref,
                     kc_ref, vc_ref):
    half = pos_ref.shape[1]
    n_rows = rk_ref.shape[2]

    def hidden(r, w1_ref):
        top = jnp.dot((r + pos_ref[0:1, :]).astype(BF16), w1_ref[0:half, :], preferred_element_type=F32)
        bot = jnp.dot((r + pos_ref[1:2, :]).astype(BF16), w1_ref[half:2 * half, :], preferred_element_type=F32)
        return jax.nn.gelu(top + pltpu.roll(bot, n_rows - 1, 0)).astype(BF16)

    kc_t = []
    vc_t = []
    for g in range(N_KV):
        k_t = lax.dot_general(w2k_ref[...], hidden(rk_ref[0, g], w1k_ref), _DN_T, preferred_element_type=F32)
        kc_t.append(_norm_rope_t(k_t, kw_ref[...], cos_ref[0], sin_ref[0]))
        vc_t.append(lax.dot_general(w2v_ref[...], hidden(rv_ref[0, g], w1v_ref), _DN_T,
                                    preferred_element_type=F32))
    kc_ref[0] = jnp.concatenate(kc_t, axis=0).T.astype(BF16)
    vc_ref[0] = jnp.concatenate(vc_t, axis=0).astype(BF16)


def _compress(kcvc, cmp_pos, k_w1, k_w2, v_w1, v_w2, k_norm_w, cos_t, sin_t):
    B, T, _ = kcvc.shape
    n_rows = T // CMP_STRIDE
    n_cmp = (T - CMP_LEN) // CMP_STRIDE + 1
    half = CMP_STRIDE * HEAD_DIM
    hid = k_w1.shape[1]

    def rows(tokens):
        return tokens.reshape(B, T, N_KV, HEAD_DIM).transpose(0, 2, 1, 3).reshape(B, N_KV, n_rows, half)

    def at_block_end(tab):
        tab = tab[:, :, CMP_LEN - 1::CMP_STRIDE]
        return jnp.pad(tab, ((0, 0), (0, 0), (0, n_rows - n_cmp)))

    blk = pl.BlockSpec((1, N_KV, n_rows, half), lambda b: (b, 0, 0, 0))
    rope = pl.BlockSpec((1, ROPE_HALF, n_rows), lambda b: (b, 0, 0))
    return pl.pallas_call(
        _compress_kernel,
        grid=(B,),
        in_specs=[blk, blk, _full((2, half)), _full((2 * half, hid)), _full((HEAD_DIM, hid)),
                  _full((2 * half, hid)), _full((HEAD_DIM, hid)), _full((HEAD_DIM, 1)), rope, rope],
        out_specs=[pl.BlockSpec((1, n_rows, KV_W), lambda b: (b, 0, 0)),
                   pl.BlockSpec((1, KV_W, n_rows), lambda b: (b, 0, 0))],
        out_shape=[jax.ShapeDtypeStruct((B, n_rows, KV_W), BF16), jax.ShapeDtypeStruct((B, KV_W, n_rows), BF16)],
        compiler_params=_params(1),
        name="compress",
    )(rows(kcvc[:, :, :KV_W]), rows(kcvc[:, :, KV_W:]), cmp_pos.reshape(2, half),
      k_w1.astype(BF16), k_w2.T.astype(BF16), v_w1.astype(BF16), v_w2.T.astype(BF16),
      k_norm_w.reshape(HEAD_DIM, 1), at_block_end(cos_t), at_block_end(sin_t))


def _attn_kernel(bound_ref, q_ref, kc_ref, vc_ref, k_ref, v_ref, gate_ref, cover_ref, ow_ref,
                 y_ref, qg_ref, s_ref, acc_ref, m_ref, bias_ref, *, n_cmp, bounded):
    tq = q_ref.shape[2]
    ck = v_ref.shape[4]
    n_rows = kc_ref.shape[1]
    n_blk = SLC_BLK
    bpc = ck // SLC_BLK
    n_back = WIN // ck
    l_rows = 16
    sel, win = 0, 1
    qi = pl.program_id(1)
    q0 = qi * tq
    neg_inf = -jnp.inf
    masked = -MASK_BIG if bounded else neg_inf
    den_floor = F32_TINY if bounded else 1e-30
    t_row = q0 + lax.broadcasted_iota(jnp.int32, (1, tq), 1)
    tile_h = lambda a: jnp.concatenate([a] * HPG, axis=1)
    col = lambda h: slice(h * tq, (h + 1) * tq)

    zero_half = jnp.zeros((HEAD_DIM, HPG * tq), BF16)
    for g in range(N_KV):
        qg = jnp.concatenate([q_ref[0, (g * HPG + h) * HEAD_DIM:(g * HPG + h + 1) * HEAD_DIM, :]
                              for h in range(HPG)], axis=1)
        qg_ref[g] = jnp.concatenate([qg, zero_half] if g == 0 else [zero_half, qg], axis=0)
    m_ref[...] = jnp.full(m_ref.shape, neg_inf, F32)
    acc_ref[...] = jnp.zeros(acc_ref.shape, F32)

    k_rel = lax.broadcasted_iota(jnp.int32, (ck, tq), 0)
    t_rel = lax.broadcasted_iota(jnp.int32, (ck, tq), 1)
    causal_bias = jnp.where(k_rel <= t_rel, 0.0, neg_inf)
    tail_bias = jnp.where(k_rel > t_rel, 0.0, neg_inf)
    ones_rows = jnp.ones((l_rows, ck), BF16)

    def half_mask(rows, g):
        lane = lax.broadcasted_iota(jnp.int32, (rows, KV_W), 1)
        return lane < HEAD_DIM if g == 0 else lane >= HEAD_DIM

    if bounded:
        spare = [HEAD_DIM, 0]
        k_lane = lax.broadcasted_iota(jnp.int32, (ck, KV_W), 1)
        k_blk = lax.broadcasted_iota(jnp.int32, (ck, KV_W), 0) // SLC_BLK
        own_half = [half_mask(ck, g) for g in range(N_KV)]
        k_spare = [jnp.where(k_lane - spare[g] == k_blk, 1.0, jnp.where(k_lane - spare[g] == bpc, 1.0, 0.0)).astype(BF16)
                   for g in range(N_KV)]
        shift_row = jnp.full((1, HPG * tq), -bound_ref[0], F32)
        pad_rows = jnp.zeros((16 - bpc - 1, HPG * tq), F32)
        zero_row = jnp.zeros((1, HPG * tq), F32)

        def set_spare_rows(g, bias_rows):
            qg_ref[g, spare[g]:spare[g] + 16, :] = jnp.concatenate(bias_rows + [shift_row, pad_rows], axis=0).astype(BF16)

    def key_chunk(br, ci):
        return k_ref[0, br, pl.ds(pl.multiple_of(ci * ck, ck), ck), :]

    def value_chunk(br, ci, g):
        return jnp.concatenate([v_ref[0, br, ci, g * HEAD_DIM:(g + 1) * HEAD_DIM, :], ones_rows], axis=0)

    def scores(k, slot, g, h):
        if bounded:
            k = jnp.where(own_half[g], k, k_spare[g])
        s_ref[slot, g, :, col(h)] = jnp.dot(k, qg_ref[g, :, col(h)], preferred_element_type=F32)

    def softmax_pv(br, slot, g, h, v_aug, bias):
        s = s_ref[slot, g, :, col(h)]
        if bias is not None:
            s = s + bias
        if bounded:
            p = jnp.exp2(s).astype(BF16)
            acc_ref[br, g, :, col(h)] += jnp.dot(v_aug, p, preferred_element_type=F32)
        else:
            m_prev = m_ref[br, g, :, col(h)]
            m_new = jnp.maximum(m_prev, jnp.max(s, axis=0, keepdims=True))
            m_safe = jnp.where(m_new == neg_inf, 0.0, m_new)
            alpha = jnp.exp2(m_prev - m_safe)
            p = jnp.exp2(s - m_safe).astype(BF16)
            acc_ref[br, g, :, col(h)] = (alpha * acc_ref[br, g, :, col(h)]
                                         + jnp.dot(v_aug, p, preferred_element_type=F32))
            m_ref[br, g, :, col(h)] = m_new

    def block_bias_rows(g, ci):
        return [bias_ref[g, pl.ds(ci * bpc + jj, 1), :] for jj in range(bpc)]

    def window_chunk(w, next_qk):
        ci = qi - n_back + w
        bias = tail_bias if w == 0 else causal_bias if w == n_back else None
        if w < n_back:
            bias = jnp.where(ci >= 0, 0.0 if bias is None else bias, neg_inf)
        for g in range(N_KV):
            v_aug = value_chunk(win, jnp.maximum(ci, 0), g)
            for h in range(HPG):
                next_qk(g, h)
                softmax_pv(win, w % 2, g, h, v_aug, bias)
                yield

    def window_early():
        if bounded:
            for g in range(N_KV):
                set_spare_rows(g, [zero_row] * bpc)
        k0 = key_chunk(win, jnp.maximum(qi - n_back, 0))
        for g in range(N_KV):
            for h in range(HPG):
                scores(k0, 0, g, h)
                yield
        for w in range(n_back):
            k_next = key_chunk(win, jnp.maximum(qi - n_back + w + 1, 0))
            yield from window_chunk(w, lambda g, h, k=k_next, slot=(w + 1) % 2: scores(k, slot, g, h))

    c_col = lax.broadcasted_iota(jnp.int32, (n_rows, tq), 0)
    cmask = ((c_col * CMP_STRIDE + (CMP_LEN - 1)) <= t_row) & (c_col < n_cmp)
    j_col = lax.broadcasted_iota(jnp.int32, (n_blk, tq), 0)
    cur = t_row // SLC_BLK
    forced = (j_col == 0) | (j_col == cur) | (j_col == cur - 1)
    valid = (j_col * SLC_BLK) <= t_row
    sub_row = lax.broadcasted_iota(jnp.int32, (8, tq), 0)
    kc = kc_ref[0]
    o_cmp = [None] * N_HEADS
    imp = [None] * N_KV
    if bounded:
        cmp_shift = jnp.where(cmask, -bound_ref[0], neg_inf)
        cmp_ones = jnp.ones((l_rows, n_rows), BF16)

    def compressed(g):
        v_cov = [vc_ref[0, g * HEAD_DIM:(g + 1) * HEAD_DIM, :], cover_ref[...]]
        if bounded:
            v_cov = jnp.concatenate(v_cov + [cmp_ones], axis=0)
            kc_g = jnp.where(half_mask(n_rows, g), kc, jnp.zeros_like(kc))
        else:
            v_cov = jnp.concatenate(v_cov, axis=0)
            kc_g = kc
        for h in range(HPG):
            s = jnp.dot(kc_g, qg_ref[g, :, col(h)], preferred_element_type=F32)
            if bounded:
                r = jnp.dot(v_cov, jnp.exp2(s + cmp_shift).astype(BF16), preferred_element_type=F32)
                r = r[0:HEAD_DIM + n_blk] / jnp.maximum(r[HEAD_DIM + n_blk:HEAD_DIM + n_blk + 1], den_floor)
            else:
                s = jnp.where(cmask, s, neg_inf)
                m = jnp.max(s, axis=0, keepdims=True)
                m = jnp.where(m == neg_inf, 0.0, m)
                e = jnp.where(cmask, jnp.exp2(s - m), 0.0)
                pb = (e / jnp.maximum(jnp.sum(e, axis=0, keepdims=True), den_floor)).astype(BF16)
                r = jnp.dot(v_cov, pb, preferred_element_type=F32)
            o_cmp[g * HPG + h] = r[0:HEAD_DIM]
            imp[g] = r[HEAD_DIM:] if imp[g] is None else imp[g] + r[HEAD_DIM:]
            yield

    def select(g):
        score = jnp.where(forced, FORCE_SCORE, jnp.where(valid, imp[g], neg_inf))
        slabs = [score[8 * sj:8 * sj + 8] for sj in range(n_blk // 8)]
        ranks = [jnp.zeros((8, tq), F32) for _ in slabs]
        for jp in range(n_blk):
            other = jnp.broadcast_to(score[jp:jp + 1, :], (8, tq))
            for sj, slab in enumerate(slabs):
                if jp < 8 * sj:
                    inc = jnp.where(other >= slab, 1.0, 0.0)
                elif jp >= 8 * sj + 8:
                    inc = jnp.where(other > slab, 1.0, 0.0)
                else:
                    tie = jnp.where(sub_row > jp - 8 * sj, 1.0, 0.0)
                    inc = jnp.where(other > slab, 1.0, jnp.where(other == slab, tie, 0.0))
                ranks[sj] = ranks[sj] + inc
            if jp % 4 == 3:
                yield
        bias_ref[g] = jnp.where(jnp.concatenate(ranks, axis=0) < SLC_TOPN, 0.0, masked)

    def interleave(main, filler, share):
        due = 0.0
        for _ in main:
            due += share
            while due >= 1.0:
                due -= 1.0
                next(filler, None)

    early = window_early()
    n_early = N_HEADS * (n_back + 1)
    n_main = N_KV * (HPG + n_blk // 4)
    for g in range(N_KV):
        interleave(compressed(g), early, n_early / n_main)
        interleave(select(g), early, n_early / n_main)
    for _ in early:
        pass

    first_slot = (n_back + 1) % 2
    if bounded:
        for g in range(N_KV):
            set_spare_rows(g, [tile_h(r) for r in block_bias_rows(g, 0)])
    k_first = key_chunk(sel, 0)
    for _ in window_chunk(n_back, lambda g, h: scores(k_first, first_slot, g, h)):
        pass

    n_sel = qi + 1

    def selected_chunk(n, slot):
        ci = jnp.minimum(n, qi)
        ci_next = jnp.minimum(n + 1, qi)
        k_next = key_chunk(sel, ci_next)
        extra = jnp.where(n >= n_sel, neg_inf, jnp.where(n == qi, causal_bias, 0.0))
        for g in range(N_KV):
            v_aug = value_chunk(sel, ci, g)
            if bounded:
                set_spare_rows(g, [tile_h(r) for r in block_bias_rows(g, ci_next)])
                bias = extra
            else:
                bias = extra + jnp.concatenate([jnp.broadcast_to(r, (SLC_BLK, tq)) for r in block_bias_rows(g, ci)],
                                               axis=0)
            for h in range(HPG):
                scores(k_next, 1 - slot, g, h)
                softmax_pv(sel, slot, g, h, v_aug, bias)

    def body(pair, carry):
        selected_chunk(2 * pair, first_slot)
        selected_chunk(2 * pair + 1, 1 - first_slot)
        return carry

    lax.fori_loop(0, (n_sel + 1) // 2, body, 0)

    def result(br):
        outs = []
        for g in range(N_KV):
            o = acc_ref[br, g, 0:HEAD_DIM] / jnp.maximum(acc_ref[br, g, HEAD_DIM:HEAD_DIM + 1], den_floor)
            outs += [o[:, col(h)] for h in range(HPG)]
        return outs

    o_slc = result(sel)
    o_win = result(win)

    gates = gate_ref[0]
    heads = []
    for idx in range(N_HEADS):
        c = idx * 3
        heads.append(gates[c:c + 1] * o_cmp[idx] + gates[c + 1:c + 2] * o_slc[idx] + gates[c + 2:c + 3] * o_win[idx])
    y = jnp.concatenate(heads, axis=0)
    y = y * lax.rsqrt(jnp.mean(y * y, axis=0, keepdims=True) + EPS) * ow_ref[...]
    y_ref[0] = y.T.astype(BF16)


def _attention(q, kc, vc, k, v, gates, out_norm_w, score_bound):
    B, _, T = q.shape
    ck = v.shape[4]
    tq = ck
    n_rows = kc.shape[1]
    n_cmp = (T - CMP_LEN) // CMP_STRIDE + 1
    nb = T // SLC_BLK
    assert nb <= SLC_BLK and WIN % ck == 0 and T % tq == 0 and ck % SLC_BLK == 0
    cs = np.arange(n_cmp)[None, :] * CMP_STRIDE
    bs = np.arange(nb)[:, None] * SLC_BLK
    cover_t = np.clip(np.minimum(cs + CMP_LEN, bs + SLC_BLK) - np.maximum(cs, bs), 0, None) / CMP_LEN
    cover_t = jnp.asarray(np.pad(cover_t, ((0, SLC_BLK - nb), (0, n_rows - n_cmp))), BF16)

    per_b = lambda *s: pl.BlockSpec((1,) + s, lambda b, i: (b,) + (0,) * len(s))

    def call(bounded):
        return pl.pallas_call(
            functools.partial(_attn_kernel, n_cmp=n_cmp, bounded=bounded),
            grid=(B, T // tq),
            in_specs=[pl.BlockSpec(memory_space=pltpu.SMEM),
                      pl.BlockSpec((1, D_ATTN, tq), lambda b, i: (b, 0, i)),
                      per_b(n_rows, KV_W), per_b(KV_W, n_rows),
                      per_b(2, T, KV_W), per_b(2, T // ck, KV_W, ck),
                      pl.BlockSpec((1, GATE_ROWS, tq), lambda b, i: (b, 0, i)),
                      _full((SLC_BLK, n_rows)), _full((D_ATTN, 1))],
            out_specs=pl.BlockSpec((1, tq, D_ATTN), lambda b, i: (b, i, 0)),
            out_shape=jax.ShapeDtypeStruct((B, T, D_ATTN), BF16),
            scratch_shapes=[pltpu.VMEM((N_KV, KV_W, HPG * tq), BF16),
                            pltpu.VMEM((2, N_KV, ck, HPG * tq), F32),
                            pltpu.VMEM((2, N_KV, HEAD_DIM + 16, HPG * tq), F32),
                            pltpu.VMEM((2, N_KV, 1, HPG * tq), F32),
                            pltpu.VMEM((N_KV, SLC_BLK, tq), F32)],
            compiler_params=_params(2),
            name="nsa_attention_bounded" if bounded else "nsa_attention",
        )(score_bound.reshape(1), q, kc, vc, k, v, gates, cover_t, out_norm_w.reshape(D_ATTN, 1))

    return lax.cond(score_bound <= MAX_SCORE_BOUND, lambda: call(True), lambda: call(False))


def _score_bound(q_norm_w, k_norm_w):
    bf16_slack = (1.0 + 2.0 ** -7) ** 2
    bound = HEAD_DIM * jnp.max(jnp.abs(q_norm_w)) * jnp.max(jnp.abs(k_norm_w)) * (Q_SCALE * bf16_slack)
    return jnp.ceil(bound.astype(F32) * 2.0) * 0.5


def _block_diag(w):
    n, d, _ = w.shape
    eye = jnp.eye(n, dtype=w.dtype)
    return (eye[:, None, :, None] * w[:, :, None, :]).reshape(n * d, n * d)


def _mixer_ffn_kernel(x_ref, ya_ref, xr_ref, gr_ref, cw_ref, cb_ref, wa_ref, ba_ref, wx_ref, bx_ref, lam_ref, rw_ref,
                      wo_r_ref, wo_a_ref, fw_ref, wg_ref, wu_ref, wd_ref,
                      o_ref, y_ref, yt_ref, ext_ref, h_ref, *, tiles_per_seq, n_tiles):
    s = pl.program_id(0)
    tm = x_ref.shape[0]
    d_ff = wg_ref.shape[1]
    pad = 8

    @pl.when(s == 0)
    def _():
        y_ref[...] = jnp.zeros(y_ref.shape, y_ref.dtype)

    @pl.when((s % tiles_per_seq == 0) & (s < n_tiles))
    def _():
        ext_ref[0:pad, :] = jnp.zeros((pad, D_RNN), F32)
        h_ref[...] = jnp.zeros((1, D_RNN), F32)

    x1 = (x_ref[...] + jnp.dot(y_ref[...], wo_r_ref[...], preferred_element_type=F32)
          + jnp.dot(ya_ref[...], wo_a_ref[...], preferred_element_type=F32))
    hb = (x1 * lax.rsqrt(jnp.mean(x1 * x1, axis=-1, keepdims=True) + EPS) * fw_ref[...]).astype(BF16)
    down = [None]

    def ffn_pieces():
        for j in range(d_ff // FFN_CHUNK):
            cols = slice(j * FFN_CHUNK, (j + 1) * FFN_CHUNK)
            gate = jnp.dot(hb, wg_ref[:, cols], preferred_element_type=F32)
            yield
            up = jnp.dot(hb, wu_ref[:, cols], preferred_element_type=F32)
            act = (gate * jax.nn.sigmoid(gate) * up).astype(BF16)
            yield
            part = jnp.dot(act, wd_ref[cols, :], preferred_element_type=F32)
            down[0] = part if down[0] is None else down[0] + part
            yield

    rb = min(MIXER_ROWS, tm)

    def mixer_pieces():
        ext_ref[pad:pad + tm, :] = xr_ref[...]
        row = lax.broadcasted_iota(jnp.int32, (rb, LANES), 0) & 7
        ssq = [None] * (tm // rb)
        for c in range(D_RNN // LANES):
            cols = slice(c * LANES, (c + 1) * LANES)
            z = -lam_ref[:, cols]
            softplus = jnp.maximum(z, 0.0) + jnp.log1p(jnp.exp(-jnp.abs(z)))
            h = h_ref[:, cols]
            for blk in range(tm // rb):
                rows = slice(blk * rb, (blk + 1) * rb)
                xc = cb_ref[:, cols]
                for k in range(CONV_WIDTH):
                    off = pad - (CONV_WIDTH - 1) + k + blk * rb
                    xc = xc + cw_ref[k:k + 1, cols] * ext_ref[off:off + rb, cols]
                xcb = xc.astype(BF16)
                r = jax.nn.sigmoid(jnp.dot(xcb, wa_ref[cols, cols], preferred_element_type=F32) + ba_ref[:, cols])
                i = jax.nn.sigmoid(jnp.dot(xcb, wx_ref[cols, cols], preferred_element_type=F32) + bx_ref[:, cols])
                yield
                log_a = -LRU_C * r * softplus
                a = jnp.exp(log_a)
                t2 = 2.0 * log_a
                u = a * a
                neg_expm1 = jnp.where(u == 1.0, -t2, (1.0 - u) * t2 / jnp.log(u))
                b = jnp.sqrt(neg_expm1) * (i * xc)
                for sh in (1, 2, 4):
                    keep = row >= sh
                    a_sh = jnp.where(keep, pltpu.roll(a, sh, 0), 1.0)
                    b_sh = jnp.where(keep, pltpu.roll(b, sh, 0), 0.0)
                    b = a * b_sh + b
                    a = a * a_sh
                yield
                hs = []
                for g in range(rb // 8):
                    hg = a[g * 8:(g + 1) * 8] * h + b[g * 8:(g + 1) * 8]
                    h = hg[7:8]
                    hs.append(hg)
                y = jax.nn.gelu(gr_ref[rows, cols]) * jnp.concatenate(hs, axis=0)
                yt_ref[rows, cols] = y
                part = jnp.sum(y * y, axis=-1, keepdims=True)
                ssq[blk] = part if ssq[blk] is None else ssq[blk] + part
                yield
            h_ref[:, cols] = h
        ext_ref[0:pad, :] = ext_ref[tm:tm + pad, :]
        scale = lax.rsqrt(jnp.concatenate(ssq, axis=0) * (1.0 / D_RNN) + EPS)
        y_ref[...] = (yt_ref[...] * scale * rw_ref[...]).astype(BF16)
        yield

    mixer = mixer_pieces()
    n_mixer = 3 * (D_RNN // LANES) * (tm // rb) + 1
    due = 0.0
    for _ in ffn_pieces():
        due += n_mixer / (3 * (d_ff // FFN_CHUNK))
        while due >= 1.0:
            due -= 1.0
            next(mixer, None)
    for _ in mixer:
        pass
    o_ref[...] = x1 + down[0]


def _mixer_ffn(x, y_att, xr, gr, conv_w, conv_b, gate_a_w, gate_a_b, gate_x_w, gate_x_b, lru_lambda, rnn_out_norm_w,
               w_out, ffn_norm_w, w_gate, w_up, w_down):
    B, T, D = x.shape
    n = B * T
    tm = min(FFN_TILE, T)
    n_tiles = n // tm
    d_ff = w_gate.shape[1]
    assert d_ff % FFN_CHUNK == 0 and T % tm == 0
    prev = lambda w: pl.BlockSpec((tm, w), lambda s: (jnp.maximum(s - 1, 0), 0))
    cur = lambda w: pl.BlockSpec((tm, w), lambda s: (jnp.minimum(s, n_tiles - 1), 0))
    once = lambda *shape: pl.BlockSpec(shape, lambda s: (0, 0), pipeline_mode=pl.Buffered(1))
    row = lambda a: a.reshape(1, -1)
    out = pl.pallas_call(
        functools.partial(_mixer_ffn_kernel, tiles_per_seq=T // tm, n_tiles=n_tiles),
        grid=(n_tiles + 1,),
        in_specs=[prev(D), prev(D_ATTN), cur(D_RNN), cur(D_RNN),
                  once(CONV_WIDTH, D_RNN), once(1, D_RNN), once(D_RNN, D_RNN), once(1, D_RNN),
                  once(D_RNN, D_RNN), once(1, D_RNN), once(1, D_RNN), once(1, D_RNN),
                  once(D_RNN, D), once(D_ATTN, D), once(1, D), once(D, d_ff), once(D, d_ff), once(d_ff, D)],
        out_specs=prev(D),
        out_shape=jax.ShapeDtypeStruct((n, D), F32),
        scratch_shapes=[pltpu.VMEM((tm, D_RNN), BF16), pltpu.VMEM((tm, D_RNN), F32),
                        pltpu.VMEM((tm + 8, D_RNN), F32), pltpu.VMEM((1, D_RNN), F32)],
        compiler_params=_params(1),
        name="mixer_ffn",
    )(x.reshape(n, D), y_att.reshape(n, D_ATTN), xr.reshape(n, D_RNN), gr.reshape(n, D_RNN),
      conv_w, row(conv_b), _block_diag(gate_a_w).astype(BF16), row(gate_a_b),
      _block_diag(gate_x_w).astype(BF16), row(gate_x_b), row(lru_lambda), row(rnn_out_norm_w),
      w_out[:D_RNN].astype(BF16), w_out[D_RNN:].astype(BF16), row(ffn_norm_w),
      w_gate.astype(BF16), w_up.astype(BF16), w_down.astype(BF16))
    return out.reshape(B, T, D)


def kernel(x, positions, attn_norm_w, w_in, conv_w, conv_b, gate_a_w, gate_a_b, gate_x_w, gate_x_b, lru_lambda, q_norm_w, k_norm_w, cmp_pos, cmp_k_w1, cmp_k_w2, cmp_v_w1, cmp_v_w2, rnn_out_norm_w, attn_out_norm_w, w_out, ffn_norm_w, w_gate, w_up, w_down):
    cos_t, sin_t = _rope_tables(positions)
    for l in range(attn_norm_w.shape[0]):
        xr, gr, kcvc, q, k, v, gates = _in_proj(x, attn_norm_w[l], w_in[l], q_norm_w[l], k_norm_w[l], cos_t, sin_t)
        kc, vc = _compress(kcvc, cmp_pos[l], cmp_k_w1[l], cmp_k_w2[l], cmp_v_w1[l], cmp_v_w2[l], k_norm_w[l],
                           cos_t, sin_t)
        y_att = _attention(q, kc, vc, k, v, gates, attn_out_norm_w[l], _score_bound(q_norm_w[l], k_norm_w[l]))
        x = _mixer_ffn(x, y_att, xr, gr, conv_w[l], conv_b[l], gate_a_w[l], gate_a_b[l], gate_x_w[l], gate_x_b[l],
                       lru_lambda[l], rnn_out_norm_w[l], w_out[l], ffn_norm_w[l], w_gate[l], w_up[l], w_down[l])
    return x
```
